```python
import jax
import jax.numpy as jnp
from jax import lax
import numpy as np

D_MODEL = 2048
BATCH = 4
SEQ = 8192
DEPTH = 4

MEM_LEN = 256
HEAD_DIM = 128
ROPE_THETA = 500000.0
ROPE_DIM = HEAD_DIM // 4
EPS = 1e-6
NEG_INF = -1e30
D_FF = 5632

GDN_HEADS = 8
GDN_DK = 128
GDN_DV = 128
GDN_QK_W = GDN_HEADS * GDN_DK
GDN_V_W = GDN_HEADS * GDN_DV
GDN_QKV_W = 2 * GDN_QK_W + GDN_V_W
GDN_CONV = 4
GDN_CHUNK = 64

DSW_HEADS = 8
DSW_W = DSW_HEADS * HEAD_DIM
DSW_PAIRS = ((128, 1), (512, 4), (2048, 16))

EVEN_IN_W = GDN_QKV_W + GDN_V_W + 2 * GDN_HEADS + 3 * DSW_W
EVEN_MIX_W = GDN_V_W + DSW_W

NSA_HEADS = 16
NSA_KV_HEADS = 4
NSA_Q_W = NSA_HEADS * HEAD_DIM
NSA_KV_W = NSA_KV_HEADS * HEAD_DIM
NSA_CMP_BLOCK = 32
NSA_CMP_STRIDE = 16
NSA_CMP_HIDDEN = 256
NSA_SLC_BLOCK = 64
NSA_SLC_TOPK = 16
NSA_WINDOW = 512
NSA_Q_BLOCK = 64
NSA_FORCE_BONUS = 1e3
ODD_IN_W = NSA_Q_W + 6 * NSA_KV_W + 3 * NSA_HEADS

XA_HEADS = 4
XA_W = XA_HEADS * HEAD_DIM

kernel_name = 'hybrid_gdn_dilated_nsa_trunk'


def rms_norm(x, g):
    xf = x.astype(jnp.float32)
    y = xf * lax.rsqrt(jnp.mean(xf * xf, axis=-1, keepdims=True) + EPS)
    return (y * g.astype(jnp.float32)).astype(x.dtype)


def l2_normalize(x):
    return x * lax.rsqrt(jnp.sum(x * x, axis=-1, keepdims=True) + EPS)


def rope_partial(x, positions):
    half = ROPE_DIM // 2
    inv_freq = jnp.float32(ROPE_THETA) ** (-jnp.arange(half, dtype=jnp.float32) / half)
    ang = positions.astype(jnp.float32)[:, :, None, None] * inv_freq
    cos, sin = jnp.cos(ang), jnp.sin(ang)
    x1 = x[..., :half].astype(jnp.float32)
    x2 = x[..., half:ROPE_DIM].astype(jnp.float32)
    rot = jnp.concatenate([x1 * cos - x2 * sin, x2 * cos + x1 * sin], axis=-1).astype(x.dtype)
    return jnp.concatenate([rot, x[..., ROPE_DIM:]], axis=-1)


def swiglu(x, w_gu, w_down):
    gate, up = jnp.split(x @ w_gu, 2, axis=-1)
    return (jax.nn.silu(gate) * up) @ w_down


def masked_softmax(s, valid):
    s = jnp.where(valid, s, NEG_INF)
    m = jnp.max(s, axis=-1, keepdims=True)
    p = jnp.where(valid, jnp.exp(s - m), 0.0)
    return p / jnp.maximum(jnp.sum(p, axis=-1, keepdims=True), 1e-30)


def causal_depthwise_conv(x, w):
    k_len = w.shape[0]
    return lax.conv_general_dilated(x, w[:, None, :], window_strides=(1,), padding=((k_len - 1, 0),),
                                    dimension_numbers=('NWC', 'WIO', 'NWC'),
                                    feature_group_count=x.shape[-1])


def gated_delta_rule(q, k, v, g, beta):
    b_, s_, h_, dk = q.shape
    dv = v.shape[-1]
    c = GDN_CHUNK
    n = s_ // c
    q = l2_normalize(q.astype(jnp.float32)) * (dk ** -0.5)
    k = l2_normalize(k.astype(jnp.float32))

    def to_chunks(t):
        return t.reshape(b_, n, c, h_, -1).transpose(0, 3, 1, 2, 4)
    q, k, v = to_chunks(q), to_chunks(k), to_chunks(v.astype(jnp.float32))
    g = g.astype(jnp.float32).reshape(b_, n, c, h_).transpose(0, 3, 1, 2)
    beta = beta.astype(jnp.float32).reshape(b_, n, c, h_).transpose(0, 3, 1, 2)
    gam = jnp.cumsum(g, axis=-1)
    causal = jnp.tril(jnp.ones((c, c), bool))
    strict = jnp.tril(jnp.ones((c, c), bool), -1)
    diff = gam[..., :, None] - gam[..., None, :]
    decay = jnp.where(causal, jnp.exp(jnp.where(causal, diff, 0.0)), 0.0)
    kb = k * beta[..., None]
    lower = jnp.where(strict, jnp.einsum('bhnid,bhnjd->bhnij', kb, k) * decay, 0.0)
    a_mat = lower + jnp.eye(c, dtype=jnp.float32)
    rhs = jnp.concatenate([v * beta[..., None], kb * jnp.exp(gam)[..., None]], axis=-1)
    sol = lax.linalg.triangular_solve(a_mat, rhs, left_side=True, lower=True, unit_diagonal=True)
    u, w = sol[..., :dv], sol[..., dv:]
    qk = jnp.where(causal, jnp.einsum('bhnid,bhnjd->bhnij', q, k) * decay, 0.0)
    q_dec = q * jnp.exp(gam)[..., None]
    k_dec = k * jnp.exp(gam[..., -1:] - gam)[..., None]
    chunk_decay = jnp.exp(gam[..., -1])

    def step(state, xs):
        qk_n, qd_n, kd_n, u_n, w_n, cd_n = xs
        v_new = u_n - jnp.einsum('bhcd,bhde->bhce', w_n, state)
        out = jnp.einsum('bhcd,bhde->bhce', qd_n, state) + jnp.einsum('bhij,bhje->bhie', qk_n, v_new)
        state = state * cd_n[..., None, None] + jnp.einsum('bhcd,bhce->bhde', kd_n, v_new)
        return state, out

    xs = tuple(jnp.moveaxis(t, 2, 0) for t in (qk, q_dec, k_dec, u, w, chunk_decay))
    state0 = jnp.zeros((b_, h_, dk, dv), jnp.float32)
    _, out = lax.scan(step, state0, xs)
    return out.transpose(1, 0, 3, 2, 4).reshape(b_, s_, h_, dv)


def dilated_branch(q, k, v, window, dil):
    b_, s_, h_, hd = q.shape
    span = window // dil
    unit = dil * span
    s_pad = -(-s_ // unit) * unit
    m_len = s_pad // dil
    nb = m_len // span

    def split(t):
        t = jnp.pad(t, ((0, 0), (0, s_pad - s_), (0, 0), (0, 0)))
        return t.reshape(b_, m_len, dil, h_, hd).transpose(0, 2, 3, 1, 4).reshape(b_, dil, h_, nb, span, hd)

    def with_prev(t):
        prev = jnp.pad(t, ((0, 0), (0, 0), (0, 0), (1, 0), (0, 0), (0, 0)))[:, :, :, :-1]
        return jnp.concatenate([prev, t], axis=4)

    qb, kb, vb = split(q), split(k), split(v)
    kk, vv = with_prev(kb), with_prev(vb)
    s = jnp.einsum('brhnqd,brhnkd->brhnqk', qb, kk, preferred_element_type=jnp.float32) * (hd ** -0.5)
    i = jnp.arange(span)[:, None]
    j = jnp.arange(2 * span)[None, :]
    dist = span + i - j
    band = (dist >= 0) & (dist <= span)
    first = (jnp.arange(nb) == 0)[:, None, None] & (j < span)[None]
    valid = band[None] & ~first
    s = jnp.where(valid, s, NEG_INF)
    m = jnp.max(s, axis=-1, keepdims=True)
    p = jnp.where(valid, jnp.exp(s - m), 0.0)
    l = jnp.sum(p, axis=-1, keepdims=True)
    num = jnp.einsum('brhnqk,brhnkd->brhnqd', p, vv.astype(jnp.float32))

    def merge(t):
        e = t.shape[-1]
        return t.reshape(b_, dil, h_, m_len, e).transpose(0, 3, 1, 2, 4).reshape(b_, s_pad, h_, e)[:, :s_]
    return merge(num), merge(m), merge(l)


def dilated_attention(q, k, v):
    outs = [dilated_branch(q, k, v, w, d) for (w, d) in DSW_PAIRS]
    m_all = jnp.max(jnp.stack([o[1] for o in outs]), axis=0)
    num = sum(jnp.exp(o[1] - m_all) * o[0] for o in outs)
    den = sum(jnp.exp(o[1] - m_all) * o[2] for o in outs)
    return (num / den).astype(q.dtype)


def compress_blocks(t, pos, w1, w2):
    b_, s_, g_, hd = t.shape
    n_cmp = (s_ - NSA_CMP_BLOCK) // NSA_CMP_STRIDE + 1
    idx = np.arange(n_cmp)[:, None] * NSA_CMP_STRIDE + np.arange(NSA_CMP_BLOCK)[None, :]
    blk = t[:, idx].transpose(0, 3, 1, 2, 4) + pos
    flat = blk.reshape(b_, g_, n_cmp, NSA_CMP_BLOCK * hd)
    return jax.nn.silu(flat @ w1) @ w2


def nsa_attention(q, kc, vc, ks, vs, kw, vw, gates):
    b_, s_, hq, hd = q.shape
    g_ = ks.shape[2]
    hpg = hq // g_
    qbl, win, ls = NSA_Q_BLOCK, NSA_WINDOW, NSA_SLC_BLOCK
    n_cmp = kc.shape[2]
    n_slc = s_ // ls
    topk = min(NSA_SLC_TOPK, n_slc)
    scale = hd ** -0.5
    qg = q.reshape(b_, s_, g_, hpg, hd)
    gg = gates.reshape(b_, s_, g_, hpg, 3)
    ks_b = ks.transpose(0, 2, 1, 3).reshape(b_, g_, n_slc, ls, hd)
    vs_b = vs.transpose(0, 2, 1, 3).reshape(b_, g_, n_slc, ls, hd)
    kw_p = jnp.pad(kw.transpose(0, 2, 1, 3), ((0, 0), (0, 0), (win, 0), (0, 0)))
    vw_p = jnp.pad(vw.transpose(0, 2, 1, 3), ((0, 0), (0, 0), (win, 0), (0, 0)))
    cmp_end = np.arange(n_cmp, dtype=np.int32) * NSA_CMP_STRIDE + (NSA_CMP_BLOCK - 1)
    c_start = np.arange(n_cmp)[:, None] * NSA_CMP_STRIDE
    s_start = np.arange(n_slc)[None, :] * ls
    overlap = jnp.asarray((c_start < s_start + ls) & (c_start + NSA_CMP_BLOCK > s_start), jnp.float32)
    b_idx = jnp.arange(b_)[:, None, None, None]
    g_idx = jnp.arange(g_)[None, :, None, None]
    blk_ids = jnp.arange(n_slc)

    def block(nb):
        t0 = nb * qbl
        tpos = t0 + jnp.arange(qbl)
        qb = lax.dynamic_slice_in_dim(qg, t0, qbl, axis=1)
        gb = lax.dynamic_slice_in_dim(gg, t0, qbl, axis=1)
        s_c = jnp.einsum('bqghd,bgnd->bgqhn', qb, kc, preferred_element_type=jnp.float32) * scale
        valid_c = (cmp_end[None, :] <= tpos[:, None])[None, None, :, None, :]
        p_c = masked_softmax(s_c, valid_c)
        o_c = jnp.einsum('bgqhn,bgnd->bqghd', p_c, vc)
        imp = jnp.einsum('bgqhn,nj->bgqj', p_c, overlap)
        cur = (tpos // ls)[:, None]
        allowed = blk_ids[None, :] * ls <= tpos[:, None]
        forced = (blk_ids[None, :] == 0) | (blk_ids[None, :] == cur) | (blk_ids[None, :] == cur - 1)
        score = jnp.where(allowed, imp + jnp.where(forced, NSA_FORCE_BONUS, 0.0), NEG_INF)
        _, sel = lax.top_k(score, topk)
        k_sel = ks_b[b_idx, g_idx, sel].reshape(b_, g_, qbl, topk * ls, hd)
        v_sel = vs_b[b_idx, g_idx, sel].reshape(b_, g_, qbl, topk * ls, hd)
        tok = (sel[..., None] * ls + jnp.arange(ls)).reshape(b_, g_, qbl, topk * ls)
        s_s = jnp.einsum('bqghd,bgqkd->bgqhk', qb, k_sel, preferred_element_type=jnp.float32) * scale
        valid_s = (tok <= tpos[None, None, :, None])[:, :, :, None, :]
        p_s = masked_softmax(s_s, valid_s)
        o_s = jnp.einsum('bgqhk,bgqkd->bqghd', p_s, v_sel)
        k_w = lax.dynamic_slice_in_dim(kw_p, t0, win + qbl, axis=2)
        v_w = lax.dynamic_slice_in_dim(vw_p, t0, win + qbl, axis=2)
        s_w = jnp.einsum('bqghd,bgkd->bgqhk', qb, k_w, preferred_element_type=jnp.float32) * scale
        jj = jnp.arange(win + qbl)[None, :]
        dist = win + jnp.arange(qbl)[:, None] - jj
        valid_w = ((dist >= 0) & (dist < win) & (jj >= win - t0))[None, None, :, None, :]
        p_w = masked_softmax(s_w, valid_w)
        o_w = jnp.einsum('bgqhk,bgkd->bqghd', p_w, v_w)
        out = gb[..., 0:1] * o_c + gb[..., 1:2] * o_s + gb[..., 2:3] * o_w
        return out.astype(q.dtype)

    outs = lax.map(block, jnp.arange(s_ // qbl))
    return outs.transpose(1, 0, 2, 3, 4, 5).reshape(b_, s_, hq, hd)


def even_mixer(h, positions, w_in, w_out, conv_w, a_log, dt_bias, gdn_norm, q_norm, k_norm):
    b_, s_, _ = h.shape
    proj = h @ w_in
    sizes = [GDN_QKV_W, GDN_V_W, GDN_HEADS, GDN_HEADS, DSW_W, DSW_W, DSW_W]
    qkv_a, z, b_logit, a_logit, q_b, k_b, v_b = jnp.split(proj, np.cumsum(sizes)[:-1].tolist(), axis=-1)

    def heads(t, d):
        return t.reshape(b_, s_, -1, d)
    qkv_a = jax.nn.silu(causal_depthwise_conv(qkv_a, conv_w))
    q_a, k_a, v_a = jnp.split(qkv_a, [GDN_QK_W, 2 * GDN_QK_W], axis=-1)
    beta = jax.nn.sigmoid(b_logit.astype(jnp.float32))
    g = -jnp.exp(a_log.astype(jnp.float32)) * jax.nn.softplus(a_logit.astype(jnp.float32) + dt_bias.astype(jnp.float32))
    o_a = gated_delta_rule(heads(q_a, GDN_DK), heads(k_a, GDN_DK), heads(v_a, GDN_DV), g, beta).astype(h.dtype)
    o_a = rms_norm(o_a, gdn_norm) * jax.nn.silu(heads(z, GDN_DV))
    qh = rope_partial(rms_norm(heads(q_b, HEAD_DIM), q_norm), positions)
    kh = rope_partial(rms_norm(heads(k_b, HEAD_DIM), k_norm), positions)
    o_b = dilated_attention(qh, kh, heads(v_b, HEAD_DIM))
    mixed = jnp.concatenate([o_a.reshape(b_, s_, -1), o_b.reshape(b_, s_, -1)], axis=-1)
    return mixed @ w_out


def odd_mixer(h, positions, w_in, w_out, q_norm, k_norm, cmp_pos, cmp_w1, cmp_w2):
    b_, s_, _ = h.shape
    proj = h @ w_in
    sizes = [NSA_Q_W] + [NSA_KV_W] * 6 + [3 * NSA_HEADS]
    q, k_c, v_c, k_s, v_s, k_w, v_w, gate_logits = jnp.split(proj, np.cumsum(sizes)[:-1].tolist(), axis=-1)

    def heads(t):
        return t.reshape(b_, s_, -1, HEAD_DIM)
    q = rope_partial(rms_norm(heads(q), q_norm), positions)
    k_c = rope_partial(rms_norm(heads(k_c), k_norm[0]), positions)
    k_s = rope_partial(rms_norm(heads(k_s), k_norm[1]), positions)
    k_w = rope_partial(rms_norm(heads(k_w), k_norm[2]), positions)
    kc_blk = compress_blocks(k_c, cmp_pos[0], cmp_w1[0], cmp_w2[0])
    vc_blk = compress_blocks(heads(v_c), cmp_pos[1], cmp_w1[1], cmp_w2[1])
    gates = jax.nn.sigmoid(gate_logits.astype(jnp.float32)).reshape(b_, s_, NSA_HEADS, 3)
    o = nsa_attention(q, kc_blk, vc_blk, k_s, heads(v_s), k_w, heads(v_w), gates)
    return o.reshape(b_, s_, -1) @ w_out


def cross_attention(h, m, w_q, w_kv, q_norm, k_norm, w_o):
    b_, s_, _ = h.shape
    m_len = m.shape[1]
    q = rms_norm((h @ w_q).reshape(b_, s_, XA_HEADS, HEAD_DIM), q_norm)
    kv = (m @ w_kv).reshape(b_, m_len, 2, XA_HEADS, HEAD_DIM)
    k = rms_norm(kv[:, :, 0], k_norm)
    v = kv[:, :, 1]
    s = jnp.einsum('bshd,bmhd->bhsm', q, k, preferred_element_type=jnp.float32) * (HEAD_DIM ** -0.5)
    p = jax.nn.softmax(s, axis=-1)
    o = jnp.einsum('bhsm,bmhd->bshd', p, v).astype(h.dtype)
    return o.reshape(b_, s_, -1) @ w_o


def setup_inputs(seed: int = 0) -> dict:
    key = jax.random.key(seed)
    keys = iter(jax.random.split(key, 64))
    f32 = jnp.float32
    n_even = (DEPTH + 1) // 2
    n_odd = DEPTH // 2

    def dense(shape, fan_in):
        return jax.random.normal(next(keys), shape, f32) * (fan_in ** -0.5)

    def gain(shape):
        return 1.0 + 0.02 * jax.random.normal(next(keys), shape, f32)

    x = jax.random.normal(next(keys), (BATCH, SEQ, D_MODEL), f32)
    mem = jax.random.normal(next(keys), (BATCH, MEM_LEN, D_MODEL), f32)
    positions = jnp.broadcast_to(jnp.arange(SEQ, dtype=jnp.int32), (BATCH, SEQ))
    dt = jnp.exp(jax.random.uniform(next(keys), (n_even, GDN_HEADS), f32, minval=float(np.log(1e-3)), maxval=float(np.log(1e-1))))
    return {
        'x': x,
        'mem': mem,
        'positions': positions,
        'ffn1_norm': gain((DEPTH, D_MODEL)),
        'ffn1_w_gu': dense((DEPTH, D_MODEL, 2 * D_FF), D_MODEL),
        'ffn1_w_down': dense((DEPTH, D_FF, D_MODEL), D_FF),
        'mix_norm': gain((DEPTH, D_MODEL)),
        'ev_w_in': dense((n_even, D_MODEL, EVEN_IN_W), D_MODEL),
        'ev_w_out': dense((n_even, EVEN_MIX_W, D_MODEL), EVEN_MIX_W),
        'gdn_conv_w': dense((n_even, GDN_CONV, GDN_QKV_W), GDN_CONV),
        'gdn_a_log': jnp.log(jax.random.uniform(next(keys), (n_even, GDN_HEADS), f32, minval=1.0, maxval=16.0)),
        'gdn_dt_bias': dt + jnp.log(-jnp.expm1(-dt)),
        'gdn_out_norm': gain((n_even, GDN_DV)),
        'dsw_q_norm': gain((n_even, HEAD_DIM)),
        'dsw_k_norm': gain((n_even, HEAD_DIM)),
        'od_w_in': dense((n_odd, D_MODEL, ODD_IN_W), D_MODEL),
        'od_w_out': dense((n_odd, NSA_Q_W, D_MODEL), NSA_Q_W),
        'nsa_q_norm': gain((n_odd, HEAD_DIM)),
        'nsa_k_norm': gain((n_odd, 3, HEAD_DIM)),
        'nsa_cmp_pos': 0.02 * jax.random.normal(next(keys), (n_odd, 2, NSA_CMP_BLOCK, HEAD_DIM), f32),
        'nsa_cmp_w1': dense((n_odd, 2, NSA_CMP_BLOCK * HEAD_DIM, NSA_CMP_HIDDEN), NSA_CMP_BLOCK * HEAD_DIM),
        'nsa_cmp_w2': dense((n_odd, 2, NSA_CMP_HIDDEN, HEAD_DIM), NSA_CMP_HIDDEN),
        'xa_norm': gain((DEPTH, D_MODEL)),
        'xa_mem_norm': gain((DEPTH, D_MODEL)),
        'xa_w_q': dense((DEPTH, D_MODEL, XA_W), D_MODEL),
        'xa_w_kv': dense((DEPTH, D_MODEL, 2 * XA_W), D_MODEL),
        'xa_q_norm': gain((DEPTH, HEAD_DIM)),
        'xa_k_norm': gain((DEPTH, HEAD_DIM)),
        'xa_w_o': dense((DEPTH, XA_W, D_MODEL), XA_W),
        'ffn2_norm': gain((DEPTH, D_MODEL)),
        'ffn2_w_gu': dense((DEPTH, D_MODEL, 2 * D_FF), D_MODEL),
        'ffn2_w_down': dense((DEPTH, D_FF, D_MODEL), D_FF),
    }


def reference(x, mem, positions, ffn1_norm, ffn1_w_gu, ffn1_w_down, mix_norm,
              ev_w_in, ev_w_out, gdn_conv_w, gdn_a_log, gdn_dt_bias, gdn_out_norm, dsw_q_norm, dsw_k_norm,
              od_w_in, od_w_out, nsa_q_norm, nsa_k_norm, nsa_cmp_pos, nsa_cmp_w1, nsa_cmp_w2,
              xa_norm, xa_mem_norm, xa_w_q, xa_w_kv, xa_q_norm, xa_k_norm, xa_w_o,
              ffn2_norm, ffn2_w_gu, ffn2_w_down):
    for i in range(DEPTH):
        x = x + 0.5 * swiglu(rms_norm(x, ffn1_norm[i]), ffn1_w_gu[i], ffn1_w_down[i])
        h = rms_norm(x, mix_norm[i])
        if i % 2 == 0:
            e = i // 2
            x = x + even_mixer(h, positions, ev_w_in[e], ev_w_out[e], gdn_conv_w[e], gdn_a_log[e],
                               gdn_dt_bias[e], gdn_out_norm[e], dsw_q_norm[e], dsw_k_norm[e])
        else:
            o = i // 2
            x = x + odd_mixer(h, positions, od_w_in[o], od_w_out[o], nsa_q_norm[o], nsa_k_norm[o],
                              nsa_cmp_pos[o], nsa_cmp_w1[o], nsa_cmp_w2[o])
        x = x + cross_attention(rms_norm(x, xa_norm[i]), rms_norm(mem, xa_mem_norm[i]), xa_w_q[i], xa_w_kv[i],
                                xa_q_norm[i], xa_k_norm[i], xa_w_o[i])
        x = x + 0.5 * swiglu(rms_norm(x, ffn2_norm[i]), ffn2_w_gu[i], ffn2_w_down[i])
    return x
```

```python
import functools

import numpy as np
import jax
import jax.numpy as jnp
from jax import lax
from jax.experimental import pallas as pl
from jax.experimental.pallas import tpu as pltpu

F32 = jnp.float32
BF16 = jnp.bfloat16
HI = lax.Precision.HIGHEST

EPS = 1e-6
NEG_INF = -1e30
HEAD_DIM = 128
ROPE_THETA = 500000.0
ROPE_DIM = HEAD_DIM // 4
ATT_SCALE = HEAD_DIM ** -0.5

GDN_HEADS = 8
GDN_CHUNK = 64
GDN_CONV = 4
DSW_HEADS = 8
DSW_PAIRS = ((128, 1), (512, 4), (2048, 16))
DSW_SPAN = 128
DSW_TILE = 2048
NSA_HEADS = 16
NSA_KV_HEADS = 4
NSA_HPG = NSA_HEADS // NSA_KV_HEADS
NSA_CMP_BLOCK = 32
NSA_CMP_STRIDE = 16
NSA_SLC_BLOCK = 64
NSA_SLC_TOPK = 16
NSA_WINDOW = 512
NSA_FORCE_BONUS = 1e3
XA_HEADS = 4

VMEM_LIMIT_BYTES = 56 * 1024 * 1024
LANES = 128


def _cparams(*sem):
    return pltpu.CompilerParams(dimension_semantics=sem, vmem_limit_bytes=VMEM_LIMIT_BYTES)


def _dot(a, b, precision=None):
    return jnp.dot(a, b, preferred_element_type=F32, precision=precision)


def _dot_nt(a, b, precision=None):
    return lax.dot_general(a, b, (((1,), (1,)), ((), ())), preferred_element_type=F32, precision=precision)


def _dot_tn(a, b, precision=None):
    return lax.dot_general(a, b, (((0,), (0,)), ((), ())), preferred_element_type=F32, precision=precision)


def _rms(x, gain):
    return x * lax.rsqrt(jnp.mean(x * x, axis=-1, keepdims=True) + EPS) * gain


def _silu(x):
    return x * jax.nn.sigmoid(x)


def _ffn_kernel(x_ref, g_ref, wg_ref, wu_ref, wd_ref, o_ref, xn_ref, acc_ref):
    f = pl.program_id(1)

    @pl.when(f == 0)
    def _():
        xn_ref[...] = _rms(x_ref[...], g_ref[...]).astype(BF16)
        acc_ref[...] = jnp.zeros_like(acc_ref)

    xn = xn_ref[...]
    gate = _dot(xn, wg_ref[...])
    up = _dot(xn, wu_ref[...])
    act = (_silu(gate) * up).astype(BF16)
    acc_ref[...] += _dot(act, wd_ref[...])

    @pl.when(f == pl.num_programs(1) - 1)
    def _():
        o_ref[...] = x_ref[...] + 0.5 * acc_ref[...]


def _ffn(x, gain, w_gu, w_down, tm=512, tf=512):
    t, d = x.shape
    d_ff = w_down.shape[0]
    n_f = d_ff // tf
    return pl.pallas_call(
        _ffn_kernel,
        grid=(t // tm, n_f),
        in_specs=[
            pl.BlockSpec((tm, d), lambda i, f: (i, 0)),
            pl.BlockSpec((1, d), lambda i, f: (0, 0)),
            pl.BlockSpec((d, tf), lambda i, f: (0, f)),
            pl.BlockSpec((d, tf), lambda i, f: (0, f + n_f)),
            pl.BlockSpec((tf, d), lambda i, f: (f, 0)),
        ],
        out_specs=pl.BlockSpec((tm, d), lambda i, f: (i, 0)),
        out_shape=jax.ShapeDtypeStruct((t, d), F32),
        scratch_shapes=[pltpu.VMEM((tm, d), BF16), pltpu.VMEM((tm, d), F32)],
        compiler_params=_cparams("parallel", "arbitrary"),
        name="ffn",
    )(x, gain, w_gu, w_gu, w_down)


def _rope_kernel(pos_ref, invf_ref, c_ref, s1_ref, s2_ref):
    ang = pos_ref[...] * invf_ref[...]
    lane = lax.broadcasted_iota(jnp.int32, ang.shape, 1)
    sin = jnp.sin(ang)
    c_ref[...] = jnp.cos(ang)
    s1_ref[...] = jnp.where(lane < ROPE_DIM // 2, -sin, 0.0)
    s2_ref[...] = jnp.where(lane >= ROPE_DIM // 2, sin, 0.0)


def _rope_tables(positions, tm=1024):
    half = ROPE_DIM // 2
    t = positions.size
    inv_freq = jnp.float32(ROPE_THETA) ** (-jnp.arange(half, dtype=F32) / half)
    invf = jnp.zeros((1, LANES), F32).at[0, :ROPE_DIM].set(jnp.concatenate([inv_freq, inv_freq]))
    pos = positions.astype(F32).reshape(t, 1)
    spec = pl.BlockSpec((tm, LANES), lambda i: (i, 0))
    return pl.pallas_call(
        _rope_kernel,
        grid=(t // tm,),
        in_specs=[pl.BlockSpec((tm, 1), lambda i: (i, 0)), pl.BlockSpec((1, LANES), lambda i: (0, 0))],
        out_specs=[spec, spec, spec],
        out_shape=[jax.ShapeDtypeStruct((t, LANES), F32)] * 3,
        compiler_params=_cparams("parallel"),
        name="rope_tables",
    )(pos, invf)


def _rope(x, c, s1, s2):
    return x * c + pltpu.roll(x, LANES - ROPE_DIM // 2, 1) * s1 + pltpu.roll(x, ROPE_DIM // 2, 1) * s2


def _norm_proj_kernel(mode, x_ref, g_ref, w_ref, *rest):
    o_ref, xn_ref = rest[-2], rest[-1]

    @pl.when(pl.program_id(1) == 0)
    def _():
        xn_ref[...] = _rms(x_ref[...], g_ref[...]).astype(BF16)

    y = _dot(xn_ref[...], w_ref[...])
    if mode == "plain":
        o_ref[...] = y
    elif mode == "sigmoid":
        o_ref[...] = jax.nn.sigmoid(y)
    elif mode == "gdn_gates":
        alog_ref, dt_ref = rest[0], rest[1]
        lane = lax.broadcasted_iota(jnp.int32, y.shape, 1)
        z = y + dt_ref[...]
        softplus = jnp.maximum(z, 0.0) + jnp.log1p(jnp.exp(-jnp.abs(z)))
        o_ref[...] = jnp.where(lane < GDN_HEADS, jax.nn.sigmoid(y), -jnp.exp(alog_ref[...]) * softplus)
    elif mode == "heads":
        hg_ref, c_ref, s1_ref, s2_ref = rest[:4]
        c, s1, s2 = c_ref[...], s1_ref[...], s2_ref[...]
        for k in range(y.shape[1] // HEAD_DIM):
            sl = slice(k * HEAD_DIM, (k + 1) * HEAD_DIM)
            o_ref[:, sl] = _rope(_rms(y[:, sl], hg_ref[:, sl]), c, s1, s2)
    else:
        raise ValueError(mode)


def _norm_proj(x, gain, w, mode, extras=(), tm=512, tn=512):
    t, d = x.shape
    n = w.shape[1]
    tn = min(tn, n)
    in_specs = [
        pl.BlockSpec((tm, d), lambda i, j: (i, 0)),
        pl.BlockSpec((1, d), lambda i, j: (0, 0)),
        pl.BlockSpec((d, tn), lambda i, j: (0, j)),
    ]
    if mode == "gdn_gates":
        in_specs += [pl.BlockSpec((1, tn), lambda i, j: (0, j))] * 2
    elif mode == "heads":
        in_specs += [pl.BlockSpec((1, tn), lambda i, j: (0, j))]
        in_specs += [pl.BlockSpec((tm, LANES), lambda i, j: (i, 0))] * 3
    return pl.pallas_call(
        functools.partial(_norm_proj_kernel, mode),
        grid=(t // tm, n // tn),
        in_specs=in_specs,
        out_specs=pl.BlockSpec((tm, tn), lambda i, j: (i, j)),
        out_shape=jax.ShapeDtypeStruct((t, n), F32),
        scratch_shapes=[pltpu.VMEM((tm, d), BF16)],
        compiler_params=_cparams("parallel", "arbitrary"),
        name="norm_proj_" + mode,
    )(x, gain, w, *extras)


def _proj_res_kernel(n_in, gated, *refs):
    x_ref = refs[0]
    a_refs = refs[1:1 + n_in]
    o_ref = refs[-1]
    w_refs = refs[1 + n_in:-2] if gated else refs[1 + n_in:-1]
    if gated:
        gt = refs[-2][...]
        cols = []
        for h in range(NSA_HEADS):
            sl = slice(h * HEAD_DIM, (h + 1) * HEAD_DIM)
            acc = None
            for j in range(n_in):
                term = gt[:, 3 * h + j:3 * h + j + 1] * a_refs[j][:, sl]
                acc = term if acc is None else acc + term
            cols.append(acc.astype(BF16))
        y = _dot(jnp.concatenate(cols, axis=-1), w_refs[0][...])
    else:
        y = None
        for a_ref, w_ref in zip(a_refs, w_refs):
            term = _dot(a_ref[...].astype(BF16), w_ref[...])
            y = term if y is None else y + term
    o_ref[...] = x_ref[...] + y


def _proj_res(x, acts, ws, gates=None, tm=512):
    t, d = x.shape
    n_in = len(acts)
    row = lambda i: (i, 0)
    in_specs = [pl.BlockSpec((tm, d), row)]
    in_specs += [pl.BlockSpec((tm, a.shape[1]), row) for a in acts]
    in_specs += [pl.BlockSpec(w.shape, lambda i: (0, 0)) for w in ws]
    args = [x, *acts, *ws]
    if gates is not None:
        in_specs.append(pl.BlockSpec((tm, LANES), row))
        args.append(gates)
    return pl.pallas_call(
        functools.partial(_proj_res_kernel, n_in, gates is not None),
        grid=(t // tm,),
        in_specs=in_specs,
        out_specs=pl.BlockSpec((tm, d), row),
        out_shape=jax.ShapeDtypeStruct((t, d), F32),
        compiler_params=_cparams("parallel"),
        name="proj_res",
    )(*args)


def _xa_kv_kernel(mem_ref, gm_ref, w_ref, kn_ref, k_ref, v_ref):
    kv = _dot(_rms(mem_ref[...], gm_ref[...]).astype(BF16), w_ref[...])
    xa_w = XA_HEADS * HEAD_DIM
    for h in range(XA_HEADS):
        sl = slice(h * HEAD_DIM, (h + 1) * HEAD_DIM)
        k_ref[:, sl] = _rms(kv[:, sl], kn_ref[...]).astype(BF16)
    v_ref[...] = kv[:, xa_w:].astype(BF16)


def _xa_kernel(x_ref, g_ref, wq_ref, qn_ref, k_ref, v_ref, wo_ref, o_ref):
    x = x_ref[...]
    q = _dot(_rms(x, g_ref[...]).astype(BF16), wq_ref[...])
    outs = []
    for h in range(XA_HEADS):
        sl = slice(h * HEAD_DIM, (h + 1) * HEAD_DIM)
        qh = (_rms(q[:, sl], qn_ref[...]) * ATT_SCALE).astype(BF16)
        s = _dot_nt(qh, k_ref[:, sl])
        p = jnp.exp(s - jnp.max(s, axis=-1, keepdims=True))
        l = jnp.sum(p, axis=-1, keepdims=True)
        outs.append((_dot(p.astype(BF16), v_ref[:, sl]) / l).astype(BF16))
    o_ref[...] = x + _dot(jnp.concatenate(outs, axis=-1), wo_ref[...])


def _cross_attention(x, mem, g_x, g_mem, w_q, w_kv, q_norm, k_norm, w_o, tm=512):
    b, s, d = x.shape
    m_len = mem.shape[1]
    xa_w = XA_HEADS * HEAD_DIM
    full = lambda shape: pl.BlockSpec(shape, lambda *_: (0,) * len(shape))
    k, v = pl.pallas_call(
        _xa_kv_kernel,
        grid=(b,),
        in_specs=[pl.BlockSpec((None, m_len, d), lambda i: (i, 0, 0)), full((1, d)), full((d, 2 * xa_w)),
                  full((1, HEAD_DIM))],
        out_specs=[pl.BlockSpec((None, m_len, xa_w), lambda i: (i, 0, 0))] * 2,
        out_shape=[jax.ShapeDtypeStruct((b, m_len, xa_w), BF16)] * 2,
        compiler_params=_cparams("parallel"),
        name="xa_kv",
    )(mem, g_mem, w_kv, k_norm)
    return pl.pallas_call(
        _xa_kernel,
        grid=(b, s // tm),
        in_specs=[pl.BlockSpec((None, tm, d), lambda i, j: (i, j, 0)), full((1, d)), full((d, xa_w)),
                  full((1, HEAD_DIM)),
                  pl.BlockSpec((None, m_len, xa_w), lambda i, j: (i, 0, 0)),
                  pl.BlockSpec((None, m_len, xa_w), lambda i, j: (i, 0, 0)),
                  full((xa_w, d))],
        out_specs=pl.BlockSpec((None, tm, d), lambda i, j: (i, j, 0)),
        out_shape=jax.ShapeDtypeStruct((b, s, d), F32),
        compiler_params=_cparams("parallel", "parallel"),
        name="cross_attn",
    )(x, g_x, w_q, q_norm, k, v, w_o)


def _gdn_kernel(q_ref, k_ref, v_ref, z_ref, sm_ref, cw_ref, gn_ref, o_ref, halo_ref, xbuf_ref, qkv_ref, state_ref):
    c = GDN_CHUNK

    @pl.when(pl.program_id(1) == 0)
    def _():
        halo_ref[...] = jnp.zeros_like(halo_ref)
        state_ref[...] = jnp.zeros_like(state_ref)

    for idx, src in enumerate((q_ref, k_ref, v_ref)):
        xbuf_ref[0:8, :] = halo_ref[idx]
        xbuf_ref[8:8 + c, :] = src[...]
        halo_ref[idx] = src[c - 8:c, :]
        w = cw_ref[idx]
        acc = xbuf_ref[8:8 + c, :] * w[GDN_CONV - 1:GDN_CONV, :]
        for tap in range(GDN_CONV - 1):
            off = 8 - (GDN_CONV - 1) + tap
            acc = acc + xbuf_ref[off:off + c, :] * w[tap:tap + 1, :]
        qkv_ref[idx] = _silu(acc)

    row = lax.broadcasted_iota(jnp.int32, (c, c), 0)
    col = lax.broadcasted_iota(jnp.int32, (c, c), 1)
    causal = row >= col
    strict = row > col
    eye = row == col
    eye_f = eye.astype(F32)
    ones = jnp.ones((c, c), F32)
    sm = sm_ref[...]
    gam_all = _dot(causal.astype(F32), sm, HI)
    for h in range(GDN_HEADS):
        sl = slice(h * HEAD_DIM, (h + 1) * HEAD_DIM)
        q = qkv_ref[0, :, sl]
        k = qkv_ref[1, :, sl]
        v = qkv_ref[2, :, sl]
        q = q * lax.rsqrt(jnp.sum(q * q, axis=-1, keepdims=True) + EPS) * ATT_SCALE
        k = k * lax.rsqrt(jnp.sum(k * k, axis=-1, keepdims=True) + EPS)
        beta = sm[:, h:h + 1]
        gam = gam_all[:, GDN_HEADS + h:GDN_HEADS + h + 1]
        gam_last = gam[c - 1:c, :]
        eg = jnp.exp(gam)
        gam_row = _dot(ones, jnp.where(eye, gam, 0.0), HI)
        decay = jnp.where(causal, jnp.exp(jnp.where(causal, gam - gam_row, 0.0)), 0.0)
        kb = k * beta
        a_mat = jnp.where(strict, _dot_nt(kb, k, HI) * decay, 0.0)
        pw = -a_mat
        inv = eye_f + pw
        for _ in range(5):
            pw = _dot(pw, pw, HI)
            inv = inv + _dot(inv, pw, HI)
        u = _dot(inv, v * beta, HI)
        w = _dot(inv, kb * eg, HI)
        qk = jnp.where(causal, _dot_nt(q, k, HI) * decay, 0.0)
        q_dec = q * eg
        k_dec = k * jnp.exp(gam_last - gam)
        state = state_ref[h]
        state_b = state.astype(BF16)
        v_new = u - _dot(w.astype(BF16), state_b)
        out = _dot(q_dec.astype(BF16), state_b) + _dot(qk.astype(BF16), v_new.astype(BF16))
        state_ref[h] = state * jnp.exp(gam_last) + _dot_tn(k_dec.astype(BF16), v_new.astype(BF16))
        o_ref[:, sl] = _rms(out, gn_ref[...]) * _silu(z_ref[:, sl])


def _gdn(proj, small, conv_w, out_norm):
    b, s, _ = proj.shape
    c = GDN_CHUNK
    w_h = GDN_HEADS * HEAD_DIM
    blk = lambda j: pl.BlockSpec((None, c, w_h), lambda bi, i: (bi, i, j))
    return pl.pallas_call(
        _gdn_kernel,
        grid=(b, s // c),
        in_specs=[blk(0), blk(1), blk(2), blk(3),
                  pl.BlockSpec((None, c, LANES), lambda bi, i: (bi, i, 0)),
                  pl.BlockSpec((3, GDN_CONV, w_h), lambda bi, i: (0, 0, 0)),
                  pl.BlockSpec((1, HEAD_DIM), lambda bi, i: (0, 0))],
        out_specs=pl.BlockSpec((None, c, w_h), lambda bi, i: (bi, i, 0)),
        out_shape=jax.ShapeDtypeStruct((b, s, w_h), F32),
        scratch_shapes=[pltpu.VMEM((3, 8, w_h), F32), pltpu.VMEM((c + 8, w_h), F32),
                        pltpu.VMEM((3, c, w_h), F32), pltpu.VMEM((GDN_HEADS, HEAD_DIM, HEAD_DIM), F32)],
        compiler_params=_cparams("parallel", "arbitrary"),
        name="gdn",
    )(proj, proj, proj, proj, small, conv_w, out_norm)


def _dsw_kernel(q_ref, kp_ref, kc_ref, vp_ref, vc_ref, o_ref, acc_ref, m_ref, l_ref):
    span = DSW_SPAN
    tile = DSW_TILE
    not_first = pl.program_id(2) > 0
    ri = lax.broadcasted_iota(jnp.int32, (span, 2 * span), 0)
    ci = lax.broadcasted_iota(jnp.int32, (span, 2 * span), 1)
    band = (ci >= ri) & (ci <= ri + span)
    band_first = band & ((ci >= span) | not_first)

    def unit(q, k, v, mask, rows, init):
        s = _dot_nt((q * ATT_SCALE).astype(BF16), k.astype(BF16))
        s = jnp.where(mask, s, NEG_INF)
        mx = jnp.max(s, axis=-1, keepdims=True)
        if init:
            p = jnp.exp(s - mx)
            m_ref[rows, :] = jnp.broadcast_to(mx, (span, LANES))
            l_ref[rows, :] = jnp.broadcast_to(jnp.sum(p, axis=-1, keepdims=True), (span, LANES))
            acc_ref[rows, :] = _dot(p.astype(BF16), v.astype(BF16))
        else:
            m_old = m_ref[rows, :]
            m_new = jnp.maximum(m_old, mx)
            alpha = jnp.exp(m_old - m_new)
            p = jnp.exp(s - m_new[:, 0:1])
            m_ref[rows, :] = m_new
            l_ref[rows, :] = alpha * l_ref[rows, :] + jnp.sum(p, axis=-1, keepdims=True)
            acc_ref[rows, :] = alpha * acc_ref[rows, :] + _dot(p.astype(BF16), v.astype(BF16))

    for bi, (window, dil) in enumerate(DSW_PAIRS):
        assert window // dil == span
        n_sub = tile // (span * dil)
        prev0 = tile - span * dil
        for r in range(dil):
            def rows_of(start, n=span):
                return pl.ds(start, n, stride=dil) if dil > 1 else pl.ds(start, n)
            k = jnp.concatenate([kp_ref[rows_of(prev0 + r), :], kc_ref[rows_of(r), :]], axis=0)
            v = jnp.concatenate([vp_ref[rows_of(prev0 + r), :], vc_ref[rows_of(r), :]], axis=0)
            unit(q_ref[rows_of(r), :], k, v, band_first, rows_of(r), bi == 0)
            for i in range(1, n_sub):
                st = r + span * dil * i
                unit(q_ref[rows_of(st), :], kc_ref[rows_of(st - span * dil, 2 * span), :],
                     vc_ref[rows_of(st - span * dil, 2 * span), :], band, rows_of(st), bi == 0)
    o_ref[...] = acc_ref[...] / l_ref[...]


def _dsw(qk, proj, v_col):
    b, s, _ = qk.shape
    tile = DSW_TILE
    cur = lambda off: pl.BlockSpec((None, tile, HEAD_DIM), lambda bi, h, t: (bi, t, off + h))
    prev = lambda off: pl.BlockSpec((None, tile, HEAD_DIM), lambda bi, h, t: (bi, jnp.maximum(t - 1, 0), off + h))
    return pl.pallas_call(
        _dsw_kernel,
        grid=(b, DSW_HEADS, s // tile),
        in_specs=[cur(0), prev(DSW_HEADS), cur(DSW_HEADS), prev(v_col), cur(v_col)],
        out_specs=cur(0),
        out_shape=jax.ShapeDtypeStruct((b, s, DSW_HEADS * HEAD_DIM), F32),
        scratch_shapes=[pltpu.VMEM((tile, HEAD_DIM), F32)] * 3,
        compiler_params=_cparams("parallel", "parallel", "arbitrary"),
        name="dsw",
    )(qk, qk, qk, proj, proj)


def _nsa_compress_kernel(x_ref, pos_ref, w1_ref, w2_ref, o_ref, shift_ref):
    st = NSA_CMP_STRIDE
    n = x_ref.shape[0] // st
    p_acc = None
    q_acc = None
    for l in range(st):
        xl = x_ref[pl.ds(l, n, stride=st), :]
        pt = _dot((xl + pos_ref[l:l + 1, :]).astype(BF16), w1_ref[l])
        qt = _dot((xl + pos_ref[st + l:st + l + 1, :]).astype(BF16), w1_ref[st + l])
        p_acc = pt if p_acc is None else p_acc + pt
        q_acc = qt if q_acc is None else q_acc + qt
    shift_ref[0:n, :] = q_acc
    shift_ref[n:n + 8, :] = jnp.zeros((8, q_acc.shape[1]), F32)
    hid = p_acc + shift_ref[pl.ds(1, n), :]
    o_ref[...] = _dot(_silu(hid).astype(BF16), w2_ref[...])


def _nsa_compress(src, col0, pos, w1, w2):
    b, s, _ = src.shape
    n = s // NSA_CMP_STRIDE
    hid = w1.shape[-1]
    return pl.pallas_call(
        _nsa_compress_kernel,
        grid=(b, NSA_KV_HEADS),
        in_specs=[pl.BlockSpec((None, s, HEAD_DIM), lambda bi, g: (bi, 0, col0 + g)),
                  pl.BlockSpec((NSA_CMP_BLOCK, HEAD_DIM), lambda bi, g: (0, 0)),
                  pl.BlockSpec((NSA_CMP_BLOCK, HEAD_DIM, hid), lambda bi, g: (0, 0, 0)),
                  pl.BlockSpec((hid, HEAD_DIM), lambda bi, g: (0, 0))],
        out_specs=pl.BlockSpec((None, None, n, HEAD_DIM), lambda bi, g: (bi, g, 0, 0)),
        out_shape=jax.ShapeDtypeStruct((b, NSA_KV_HEADS, n, HEAD_DIM), F32),
        scratch_shapes=[pltpu.VMEM((n + 8, hid), F32)],
        compiler_params=_cparams("parallel", "parallel"),
        name="nsa_compress",
    )(src, pos, w1, w2)


def _nsa_cmp_kernel(q_ref, kc_ref, vc_ref, ov_ref, oc_ref, imp_ref):
    tq = q_ref.shape[0]
    ncp = kc_ref.shape[0]
    t0 = pl.program_id(2) * tq
    tpos = t0 + lax.broadcasted_iota(jnp.int32, (tq, ncp), 0)
    cend = lax.broadcasted_iota(jnp.int32, (tq, ncp), 1) * NSA_CMP_STRIDE + (NSA_CMP_BLOCK - 1)
    valid = cend <= tpos
    kc = kc_ref[...].astype(BF16)
    vc = vc_ref[...].astype(BF16)
    psum = None
    for h in range(NSA_HPG):
        sl = slice(h * HEAD_DIM, (h + 1) * HEAD_DIM)
        s = jnp.where(valid, _dot_nt((q_ref[:, sl] * ATT_SCALE).astype(BF16), kc), NEG_INF)
        p = jnp.where(valid, jnp.exp(s - jnp.max(s, axis=-1, keepdims=True)), 0.0)
        p = p / jnp.maximum(jnp.sum(p, axis=-1, keepdims=True), 1e-30)
        oc_ref[:, sl] = _dot(p.astype(BF16), vc)
        psum = p if psum is None else psum + p
    imp_ref[...] = _dot(psum, ov_ref[...], HI)


def _nsa_select_kernel(imp_ref, sel_ref):
    tr = imp_ref.shape[0]
    t0 = pl.program_id(2) * tr
    tpos = t0 + lax.broadcasted_iota(jnp.int32, (tr, LANES), 0)
    j = lax.broadcasted_iota(jnp.int32, (tr, LANES), 1)
    cur = tpos // NSA_SLC_BLOCK
    allowed = j <= cur
    forced = (j == 0) | (j == cur) | (j == cur - 1)
    score = jnp.where(allowed, imp_ref[...] + jnp.where(forced, NSA_FORCE_BONUS, 0.0), NEG_INF)
    sel = jnp.zeros((tr, LANES), F32)
    for _ in range(NSA_SLC_TOPK):
        top = jnp.max(score, axis=-1, keepdims=True)
        idx = jnp.min(jnp.where(score == top, j, LANES), axis=-1, keepdims=True)
        hit = j == idx
        sel = jnp.where(hit, 1.0, sel)
        score = jnp.where(hit, -3e38, score)
    sel_ref[...] = sel


def _nsa_slc_kernel(q_ref, k_ref, v_ref, sel_ref, o_ref, acc_ref, m_ref, l_ref):
    tq = q_ref.shape[0]
    tk = 512
    bpt = tk // NSA_SLC_BLOCK
    t0 = pl.program_id(2) * tq
    n_kv = (t0 + tq + tk - 1) // tk
    acc_ref[...] = jnp.zeros_like(acc_ref)
    l_ref[...] = jnp.zeros_like(l_ref)
    m_ref[...] = jnp.full(m_ref.shape, NEG_INF, F32)
    sel = sel_ref[...].astype(BF16)
    tpos = t0 + lax.broadcasted_iota(jnp.int32, (tq, tk), 0)
    kcol = lax.broadcasted_iota(jnp.int32, (tq, tk), 1)
    eb = lax.broadcasted_iota(jnp.int32, (LANES, tk), 0)
    ec = lax.broadcasted_iota(jnp.int32, (LANES, tk), 1) // NSA_SLC_BLOCK

    def body(jt, carry):
        k0 = pl.multiple_of(jt * tk, tk)
        kt = k_ref[pl.ds(k0, tk), :].astype(BF16)
        vt = v_ref[pl.ds(k0, tk), :].astype(BF16)
        expand = jnp.where(eb == jt * bpt + ec, 1.0, 0.0).astype(BF16)
        valid = (_dot(sel, expand) > 0.5) & (k0 + kcol <= tpos)
        for h in range(NSA_HPG):
            sl = slice(h * HEAD_DIM, (h + 1) * HEAD_DIM)
            s = jnp.where(valid, _dot_nt((q_ref[:, sl] * ATT_SCALE).astype(BF16), kt), NEG_INF)
            m_old = m_ref[h]
            m_new = jnp.maximum(m_old, jnp.max(s, axis=-1, keepdims=True))
            alpha = jnp.exp(m_old - m_new)
            p = jnp.where(valid, jnp.exp(s - m_new[:, 0:1]), 0.0)
            m_ref[h] = m_new
            l_ref[h] = alpha * l_ref[h] + jnp.sum(p, axis=-1, keepdims=True)
            acc_ref[h] = alpha * acc_ref[h] + _dot(p.astype(BF16), vt)
        return carry

    lax.fori_loop(0, n_kv, body, 0)
    for h in range(NSA_HPG):
        o_ref[:, h * HEAD_DIM:(h + 1) * HEAD_DIM] = acc_ref[h] / jnp.maximum(l_ref[h], 1e-30)


def _nsa_win_kernel(q_ref, k_ref, v_ref, o_ref):
    tq = q_ref.shape[0]
    nk = NSA_WINDOW + tq
    t0 = pl.program_id(2) * tq
    start = pl.multiple_of(jnp.maximum(t0 - NSA_WINDOW, 0), tq)
    kt = k_ref[pl.ds(start, nk), :].astype(BF16)
    vt = v_ref[pl.ds(start, nk), :].astype(BF16)
    tpos = t0 + lax.broadcasted_iota(jnp.int32, (tq, nk), 0)
    kpos = start + lax.broadcasted_iota(jnp.int32, (tq, nk), 1)
    valid = (kpos <= tpos) & (kpos > tpos - NSA_WINDOW)
    for h in range(NSA_HPG):
        sl = slice(h * HEAD_DIM, (h + 1) * HEAD_DIM)
        s = jnp.where(valid, _dot_nt((q_ref[:, sl] * ATT_SCALE).astype(BF16), kt), NEG_INF)
        p = jnp.where(valid, jnp.exp(s - jnp.max(s, axis=-1, keepdims=True)), 0.0)
        l = jnp.sum(p, axis=-1, keepdims=True)
        o_ref[:, sl] = _dot(p.astype(BF16), vt) / jnp.maximum(l, 1e-30)


def _nsa_overlap(n_cmp_pad, n_cmp):
    c_start = np.arange(n_cmp_pad)[:, None] * NSA_CMP_STRIDE
    s_start = np.arange(LANES)[None, :] * NSA_SLC_BLOCK
    ov = (c_start < s_start + NSA_SLC_BLOCK) & (c_start + NSA_CMP_BLOCK > s_start)
    ov &= (np.arange(n_cmp_pad) < n_cmp)[:, None]
    return jnp.asarray(ov, F32)


def _nsa_attention(hd, pv, kc, vc, tq=256):
    b, s, _ = hd.shape
    g = NSA_KV_HEADS
    qw = NSA_HPG * HEAD_DIM
    ncp = s // NSA_CMP_STRIDE
    assert s // NSA_SLC_BLOCK <= LANES and s >= NSA_WINDOW + tq
    grid = (b, g, s // tq)
    q_spec = pl.BlockSpec((None, tq, qw), lambda bi, gi, i: (bi, i, gi))
    seq_spec = lambda off: pl.BlockSpec((None, s, HEAD_DIM), lambda bi, gi, i: (bi, 0, off + gi))
    cmp_spec = pl.BlockSpec((None, None, ncp, HEAD_DIM), lambda bi, gi, i: (bi, gi, 0, 0))
    row_spec = pl.BlockSpec((None, None, tq, LANES), lambda bi, gi, i: (bi, gi, i, 0))
    out_sd = jax.ShapeDtypeStruct((b, s, NSA_HEADS * HEAD_DIM), F32)
    row_sd = jax.ShapeDtypeStruct((b, g, s, LANES), F32)
    sem = ("parallel", "parallel", "arbitrary")
    k_slc, k_win = NSA_HEADS + g, NSA_HEADS + 2 * g
    v_slc, v_win = g, 2 * g

    o_c, imp = pl.pallas_call(
        _nsa_cmp_kernel, grid=grid,
        in_specs=[q_spec, cmp_spec, cmp_spec, pl.BlockSpec((ncp, LANES), lambda bi, gi, i: (0, 0))],
        out_specs=[q_spec, row_spec], out_shape=[out_sd, row_sd],
        compiler_params=_cparams(*sem), name="nsa_cmp",
    )(hd, kc, vc, _nsa_overlap(ncp, ncp - 1))
    sel = pl.pallas_call(
        _nsa_select_kernel, grid=grid, in_specs=[row_spec], out_specs=row_spec, out_shape=row_sd,
        compiler_params=_cparams(*sem), name="nsa_select",
    )(imp)
    o_s = pl.pallas_call(
        _nsa_slc_kernel, grid=grid,
        in_specs=[q_spec, seq_spec(k_slc), seq_spec(v_slc), row_spec],
        out_specs=q_spec, out_shape=out_sd,
        scratch_shapes=[pltpu.VMEM((NSA_HPG, tq, HEAD_DIM), F32)] * 3,
        compiler_params=_cparams(*sem), name="nsa_slc",
    )(hd, hd, pv, sel)
    o_w = pl.pallas_call(
        _nsa_win_kernel, grid=grid,
        in_specs=[q_spec, seq_spec(k_win), seq_spec(v_win)],
        out_specs=q_spec, out_shape=out_sd,
        compiler_params=_cparams(*sem), name="nsa_win",
    )(hd, hd, pv)
    return o_c, o_s, o_w


def _row(v, width=None):
    v = v.reshape(1, -1).astype(F32)
    if width is not None and v.shape[1] < width:
        v = jnp.pad(v, ((0, 0), (0, width - v.shape[1])))
    return v


def _even_mixer(xf, b, s, gain, w_in, w_out, conv_w, a_log, dt_bias, gdn_norm, q_norm, k_norm, rope):
    gw = GDN_HEADS * HEAD_DIM
    dw = DSW_HEADS * HEAD_DIM
    o_small = 4 * gw
    o_dsw = o_small + 2 * GDN_HEADS
    w_plain = jnp.concatenate([w_in[:, :o_small], w_in[:, o_dsw + 2 * dw:]], axis=1).astype(BF16)
    w_heads = w_in[:, o_dsw:o_dsw + 2 * dw].astype(BF16)
    w_small = jnp.pad(w_in[:, o_small:o_dsw], ((0, 0), (0, LANES - 2 * GDN_HEADS))).astype(BF16)
    pad8 = lambda v: jnp.pad(_row(v), ((0, 0), (GDN_HEADS, LANES - 2 * GDN_HEADS)))
    head_gain = jnp.concatenate([jnp.tile(_row(q_norm), (1, DSW_HEADS)), jnp.tile(_row(k_norm), (1, DSW_HEADS))], axis=1)

    proj = _norm_proj(xf, gain, w_plain, "plain")
    small = _norm_proj(xf, gain, w_small, "gdn_gates", (pad8(a_log), pad8(dt_bias)))
    qk = _norm_proj(xf, gain, w_heads, "heads", (head_gain, *rope))
    cw = conv_w.reshape(GDN_CONV, 3, gw).transpose(1, 0, 2)
    o_a = _gdn(proj.reshape(b, s, -1), small.reshape(b, s, LANES), cw, _row(gdn_norm))
    o_b = _dsw(qk.reshape(b, s, -1), proj.reshape(b, s, -1), o_small // HEAD_DIM)
    t = b * s
    return _proj_res(xf, [o_a.reshape(t, gw), o_b.reshape(t, dw)],
                     [w_out[:gw].astype(BF16), w_out[gw:].astype(BF16)])


def _odd_mixer(xf, b, s, gain, w_in, w_out, q_norm, k_norm, cmp_pos, cmp_w1, cmp_w2, rope):
    qw = NSA_HEADS * HEAD_DIM
    kvw = NSA_KV_HEADS * HEAD_DIM
    col = lambda i: slice(qw + i * kvw, qw + (i + 1) * kvw)
    w_heads = jnp.concatenate([w_in[:, :qw], w_in[:, col(0)], w_in[:, col(2)], w_in[:, col(4)]], axis=1).astype(BF16)
    w_plain = jnp.concatenate([w_in[:, col(1)], w_in[:, col(3)], w_in[:, col(5)]], axis=1).astype(BF16)
    w_gate = jnp.pad(w_in[:, qw + 6 * kvw:], ((0, 0), (0, LANES - 3 * NSA_HEADS))).astype(BF16)
    head_gain = jnp.concatenate([jnp.tile(_row(q_norm), (1, NSA_HEADS))]
                                + [jnp.tile(_row(k_norm[i]), (1, NSA_KV_HEADS)) for i in range(3)], axis=1)

    hd = _norm_proj(xf, gain, w_heads, "heads", (head_gain, *rope)).reshape(b, s, -1)
    pv = _norm_proj(xf, gain, w_plain, "plain").reshape(b, s, -1)
    gates = _norm_proj(xf, gain, w_gate, "sigmoid")
    hid = cmp_w1.shape[-1]
    w1 = cmp_w1.reshape(2, NSA_CMP_BLOCK, HEAD_DIM, hid).astype(BF16)
    kc = _nsa_compress(hd, NSA_HEADS, cmp_pos[0], w1[0], cmp_w2[0].astype(BF16))
    vc = _nsa_compress(pv, 0, cmp_pos[1], w1[1], cmp_w2[1].astype(BF16))
    o_c, o_s, o_w = _nsa_attention(hd, pv, kc, vc)
    t = b * s
    return _proj_res(xf, [o_c.reshape(t, qw), o_s.reshape(t, qw), o_w.reshape(t, qw)], [w_out.astype(BF16)], gates)


def kernel(x, mem, positions, ffn1_norm, ffn1_w_gu, ffn1_w_down, mix_norm, ev_w_in, ev_w_out, gdn_conv_w, gdn_a_log, gdn_dt_bias, gdn_out_norm, dsw_q_norm, dsw_k_norm, od_w_in, od_w_out, nsa_q_norm, nsa_k_norm, nsa_cmp_pos, nsa_cmp_w1, nsa_cmp_w2, xa_norm, xa_mem_norm, xa_w_q, xa_w_kv, xa_q_norm, xa_k_norm, xa_w_o, ffn2_norm, ffn2_w_gu, ffn2_w_down):
    b, s, d = x.shape
    depth = ffn1_norm.shape[0]
    rope = _rope_tables(positions)
    xf = x.reshape(b * s, d)
    for i in range(depth):
        xf = _ffn(xf, _row(ffn1_norm[i]), ffn1_w_gu[i].astype(BF16), ffn1_w_down[i].astype(BF16))
        if i % 2 == 0:
            e = i // 2
            xf = _even_mixer(xf, b, s, _row(mix_norm[i]), ev_w_in[e], ev_w_out[e], gdn_conv_w[e], gdn_a_log[e],
                             gdn_dt_bias[e], gdn_out_norm[e], dsw_q_norm[e], dsw_k_norm[e], rope)
        else:
            o = i // 2
            xf = _odd_mixer(xf, b, s, _row(mix_norm[i]), od_w_in[o], od_w_out[o], nsa_q_norm[o], nsa_k_norm[o],
                            nsa_cmp_pos[o], nsa_cmp_w1[o], nsa_cmp_w2[o], rope)
        xf = _cross_attention(xf.reshape(b, s, d), mem, _row(xa_norm[i]), _row(xa_mem_norm[i]),
                              xa_w_q[i].astype(BF16), xa_w_kv[i].astype(BF16), _row(xa_q_norm[i]),
                              _row(xa_k_norm[i]), xa_w_o[i].astype(BF16)).reshape(b * s, d)
        xf = _ffn(xf, _row(ffn2_norm[i]), ffn2_w_gu[i].astype(BF16), ffn2_w_down[i].astype(BF16))
    return xf.reshape(b, s, d)
```

```python
import functools

import numpy as np
import jax
import jax.numpy as jnp
from jax import lax
from jax.experimental import pallas as pl
from jax.experimental.pallas import tpu as pltpu

F32 = jnp.float32
BF16 = jnp.bfloat16
HI = lax.Precision.HIGHEST

EPS = 1e-6
NEG_INF = -1e30
HEAD_DIM = 128
ROPE_THETA = 500000.0
ROPE_DIM = HEAD_DIM // 4
ATT_SCALE = HEAD_DIM ** -0.5
LOG2E = float(np.log2(np.e))
Q_SCALE_LOG2 = ATT_SCALE * LOG2E

GDN_HEADS = 8
GDN_CHUNK = 64
GDN_CONV = 4
DSW_HEADS = 8
DSW_PAIRS = ((128, 1), (512, 4), (2048, 16))
DSW_SPAN = 128
DSW_TILE = 2048
NSA_HEADS = 16
NSA_KV_HEADS = 4
NSA_HPG = NSA_HEADS // NSA_KV_HEADS
NSA_CMP_BLOCK = 32
NSA_CMP_STRIDE = 16
NSA_SLC_BLOCK = 64
NSA_SLC_TOPK = 16
NSA_WINDOW = 512
NSA_FORCE_BONUS = 1e3
XA_HEADS = 4

VMEM_LIMIT_BYTES = 56 * 1024 * 1024
LANES = 128


def _cparams(*sem):
    return pltpu.CompilerParams(dimension_semantics=sem, vmem_limit_bytes=VMEM_LIMIT_BYTES)


def _dot(a, b, precision=None):
    return jnp.dot(a, b, preferred_element_type=F32, precision=precision)


def _dot_nt(a, b, precision=None):
    return lax.dot_general(a, b, (((1,), (1,)), ((), ())), preferred_element_type=F32, precision=precision)


def _dot_tn(a, b, precision=None):
    return lax.dot_general(a, b, (((0,), (0,)), ((), ())), preferred_element_type=F32, precision=precision)


def _bmm(a, b, dn):
    return lax.dot_general(a.astype(BF16), b.astype(BF16), dn, preferred_element_type=F32)


def _bdot(a, b):
    return _bmm(a, b, (((2,), (1,)), ((0,), (0,))))


def _bdot_nt(a, b):
    return _bmm(a, b, (((2,), (2,)), ((0,), (0,))))


def _bdot_tn(a, b):
    return _bmm(a, b, (((1,), (1,)), ((0,), (0,))))


def _rms(x, gain):
    return x * lax.rsqrt(jnp.mean(x * x, axis=-1, keepdims=True) + EPS) * gain


def _silu(x):
    return x * jax.nn.sigmoid(x)


def _ffn_kernel(x_ref, g_ref, wg_ref, wu_ref, wd_ref, o_ref, xn_ref, acc_ref):
    f = pl.program_id(1)

    @pl.when(f == 0)
    def _():
        xn_ref[...] = _rms(x_ref[...], g_ref[...]).astype(BF16)
        acc_ref[...] = jnp.zeros_like(acc_ref)

    xn = xn_ref[...]
    gate = _dot(xn, wg_ref[...])
    up = _dot(xn, wu_ref[...])
    act = (_silu(gate) * up).astype(BF16)
    acc_ref[...] += _dot(act, wd_ref[...])

    @pl.when(f == pl.num_programs(1) - 1)
    def _():
        o_ref[...] = x_ref[...] + 0.5 * acc_ref[...]


def _ffn(x, gain, w_gu, w_down, tm=512, tf=512):
    t, d = x.shape
    d_ff = w_down.shape[0]
    n_f = d_ff // tf
    return pl.pallas_call(
        _ffn_kernel,
        grid=(t // tm, n_f),
        in_specs=[
            pl.BlockSpec((tm, d), lambda i, f: (i, 0)),
            pl.BlockSpec((1, d), lambda i, f: (0, 0)),
            pl.BlockSpec((d, tf), lambda i, f: (0, f)),
            pl.BlockSpec((d, tf), lambda i, f: (0, f + n_f)),
            pl.BlockSpec((tf, d), lambda i, f: (f, 0)),
        ],
        out_specs=pl.BlockSpec((tm, d), lambda i, f: (i, 0)),
        out_shape=jax.ShapeDtypeStruct((t, d), F32),
        scratch_shapes=[pltpu.VMEM((tm, d), BF16), pltpu.VMEM((tm, d), F32)],
        compiler_params=_cparams("parallel", "arbitrary"),
        name="ffn",
    )(x, gain, w_gu, w_gu, w_down)


def _rope_kernel(pos_ref, invf_ref, c_ref, s1_ref, s2_ref):
    ang = pos_ref[...] * invf_ref[...]
    lane = lax.broadcasted_iota(jnp.int32, ang.shape, 1)
    sin = jnp.sin(ang)
    c_ref[...] = jnp.cos(ang)
    s1_ref[...] = jnp.where(lane < ROPE_DIM // 2, -sin, 0.0)
    s2_ref[...] = jnp.where(lane >= ROPE_DIM // 2, sin, 0.0)


def _rope_tables(positions, tm=1024):
    half = ROPE_DIM // 2
    t = positions.size
    inv_freq = jnp.float32(ROPE_THETA) ** (-jnp.arange(half, dtype=F32) / half)
    invf = jnp.zeros((1, LANES), F32).at[0, :ROPE_DIM].set(jnp.concatenate([inv_freq, inv_freq]))
    pos = positions.astype(F32).reshape(t, 1)
    spec = pl.BlockSpec((tm, LANES), lambda i: (i, 0))
    return pl.pallas_call(
        _rope_kernel,
        grid=(t // tm,),
        in_specs=[pl.BlockSpec((tm, 1), lambda i: (i, 0)), pl.BlockSpec((1, LANES), lambda i: (0, 0))],
        out_specs=[spec, spec, spec],
        out_shape=[jax.ShapeDtypeStruct((t, LANES), F32)] * 3,
        compiler_params=_cparams("parallel"),
        name="rope_tables",
    )(pos, invf)


def _rope(x, c, s1, s2):
    return x * c + pltpu.roll(x, LANES - ROPE_DIM // 2, 1) * s1 + pltpu.roll(x, ROPE_DIM // 2, 1) * s2


def _norm_proj_kernel(mode, x_ref, g_ref, w_ref, *rest):
    o_ref, xn_ref = rest[-2], rest[-1]

    @pl.when(pl.program_id(1) == 0)
    def _():
        xn_ref[...] = _rms(x_ref[...], g_ref[...]).astype(BF16)

    y = _dot(xn_ref[...], w_ref[...])
    if mode == "plain":
        o_ref[...] = y.astype(o_ref.dtype)
    elif mode == "sigmoid":
        o_ref[...] = jax.nn.sigmoid(y)
    elif mode == "gdn_gates":
        alog_ref, dt_ref = rest[0], rest[1]
        lane = lax.broadcasted_iota(jnp.int32, y.shape, 1)
        z = y + dt_ref[...]
        softplus = jnp.maximum(z, 0.0) + jnp.log1p(jnp.exp(-jnp.abs(z)))
        o_ref[...] = jnp.where(lane < GDN_HEADS, jax.nn.sigmoid(y), -jnp.exp(alog_ref[...]) * softplus)
    elif mode == "heads":
        hg_ref, c_ref, s1_ref, s2_ref = rest[:4]
        c, s1, s2 = c_ref[...], s1_ref[...], s2_ref[...]
        for k in range(y.shape[1] // HEAD_DIM):
            sl = slice(k * HEAD_DIM, (k + 1) * HEAD_DIM)
            o_ref[:, sl] = _rope(_rms(y[:, sl], hg_ref[:, sl]), c, s1, s2).astype(o_ref.dtype)
    else:
        raise ValueError(mode)


def _norm_proj(x, gain, w, mode, extras=(), out_dtype=F32, tm=512, tn=512):
    t, d = x.shape
    n = w.shape[1]
    tn = min(tn, n)
    in_specs = [
        pl.BlockSpec((tm, d), lambda i, j: (i, 0)),
        pl.BlockSpec((1, d), lambda i, j: (0, 0)),
        pl.BlockSpec((d, tn), lambda i, j: (0, j)),
    ]
    if mode == "gdn_gates":
        in_specs += [pl.BlockSpec((1, tn), lambda i, j: (0, j))] * 2
    elif mode == "heads":
        in_specs += [pl.BlockSpec((1, tn), lambda i, j: (0, j))]
        in_specs += [pl.BlockSpec((tm, LANES), lambda i, j: (i, 0))] * 3
    return pl.pallas_call(
        functools.partial(_norm_proj_kernel, mode),
        grid=(t // tm, n // tn),
        in_specs=in_specs,
        out_specs=pl.BlockSpec((tm, tn), lambda i, j: (i, j)),
        out_shape=jax.ShapeDtypeStruct((t, n), out_dtype),
        scratch_shapes=[pltpu.VMEM((tm, d), BF16)],
        compiler_params=_cparams("parallel", "arbitrary"),
        name="norm_proj_" + mode,
    )(x, gain, w, *extras)


def _proj_res_kernel(n_in, gated, *refs):
    x_ref = refs[0]
    a_refs = refs[1:1 + n_in]
    o_ref = refs[-1]
    w_refs = refs[1 + n_in:-2] if gated else refs[1 + n_in:-1]
    if gated:
        gt = refs[-2][...]
        cols = []
        for h in range(NSA_HEADS):
            sl = slice(h * HEAD_DIM, (h + 1) * HEAD_DIM)
            acc = None
            for j in range(n_in):
                term = gt[:, 3 * h + j:3 * h + j + 1] * a_refs[j][:, sl]
                acc = term if acc is None else acc + term
            cols.append(acc.astype(BF16))
        y = _dot(jnp.concatenate(cols, axis=-1), w_refs[0][...])
    else:
        y = None
        for a_ref, w_ref in zip(a_refs, w_refs):
            term = _dot(a_ref[...].astype(BF16), w_ref[...])
            y = term if y is None else y + term
    o_ref[...] = x_ref[...] + y


def _proj_res(x, acts, ws, gates=None, tm=512):
    t, d = x.shape
    n_in = len(acts)
    row = lambda i: (i, 0)
    in_specs = [pl.BlockSpec((tm, d), row)]
    in_specs += [pl.BlockSpec((tm, a.shape[1]), row) for a in acts]
    in_specs += [pl.BlockSpec(w.shape, lambda i: (0, 0)) for w in ws]
    args = [x, *acts, *ws]
    if gates is not None:
        in_specs.append(pl.BlockSpec((tm, LANES), row))
        args.append(gates)
    return pl.pallas_call(
        functools.partial(_proj_res_kernel, n_in, gates is not None),
        grid=(t // tm,),
        in_specs=in_specs,
        out_specs=pl.BlockSpec((tm, d), row),
        out_shape=jax.ShapeDtypeStruct((t, d), F32),
        compiler_params=_cparams("parallel"),
        name="proj_res",
    )(*args)


def _xa_kv_kernel(mem_ref, gm_ref, w_ref, kn_ref, k_ref, v_ref):
    kv = _dot(_rms(mem_ref[...], gm_ref[...]).astype(BF16), w_ref[...])
    xa_w = XA_HEADS * HEAD_DIM
    for h in range(XA_HEADS):
        sl = slice(h * HEAD_DIM, (h + 1) * HEAD_DIM)
        k_ref[:, sl] = _rms(kv[:, sl], kn_ref[...]).astype(BF16)
    v_ref[...] = kv[:, xa_w:].astype(BF16)


def _xa_kernel(x_ref, g_ref, wq_ref, qn_ref, k_ref, v_ref, wo_ref, o_ref):
    x = x_ref[...]
    q = _dot(_rms(x, g_ref[...]).astype(BF16), wq_ref[...])
    outs = []
    for h in range(XA_HEADS):
        sl = slice(h * HEAD_DIM, (h + 1) * HEAD_DIM)
        qh = (_rms(q[:, sl], qn_ref[...]) * Q_SCALE_LOG2).astype(BF16)
        s = _dot_nt(qh, k_ref[:, sl])
        p = jnp.exp2(s - jnp.max(s, axis=-1, keepdims=True))
        l = jnp.sum(p, axis=-1, keepdims=True)
        outs.append((_dot(p.astype(BF16), v_ref[:, sl]) / l).astype(BF16))
    o_ref[...] = x + _dot(jnp.concatenate(outs, axis=-1), wo_ref[...])


def _cross_attention(x, mem, g_x, g_mem, w_q, w_kv, q_norm, k_norm, w_o, tm=512):
    b, s, d = x.shape
    m_len = mem.shape[1]
    xa_w = XA_HEADS * HEAD_DIM
    full = lambda shape: pl.BlockSpec(shape, lambda *_: (0,) * len(shape))
    k, v = pl.pallas_call(
        _xa_kv_kernel,
        grid=(b,),
        in_specs=[pl.BlockSpec((None, m_len, d), lambda i: (i, 0, 0)), full((1, d)), full((d, 2 * xa_w)),
                  full((1, HEAD_DIM))],
        out_specs=[pl.BlockSpec((None, m_len, xa_w), lambda i: (i, 0, 0))] * 2,
        out_shape=[jax.ShapeDtypeStruct((b, m_len, xa_w), BF16)] * 2,
        compiler_params=_cparams("parallel"),
        name="xa_kv",
    )(mem, g_mem, w_kv, k_norm)
    return pl.pallas_call(
        _xa_kernel,
        grid=(b, s // tm),
        in_specs=[pl.BlockSpec((None, tm, d), lambda i, j: (i, j, 0)), full((1, d)), full((d, xa_w)),
                  full((1, HEAD_DIM)),
                  pl.BlockSpec((None, m_len, xa_w), lambda i, j: (i, 0, 0)),
                  pl.BlockSpec((None, m_len, xa_w), lambda i, j: (i, 0, 0)),
                  full((xa_w, d))],
        out_specs=pl.BlockSpec((None, tm, d), lambda i, j: (i, j, 0)),
        out_shape=jax.ShapeDtypeStruct((b, s, d), F32),
        compiler_params=_cparams("parallel", "parallel"),
        name="cross_attn",
    )(x, g_x, w_q, q_norm, k, v, w_o)


def _gdn_kernel(q_ref, k_ref, v_ref, z_ref, sm_ref, cw_ref, gn_ref, o_ref, halo_ref, xbuf_ref, qkv_ref, state_ref):
    c = GDN_CHUNK

    @pl.when(pl.program_id(1) == 0)
    def _():
        halo_ref[...] = jnp.zeros_like(halo_ref)
        state_ref[...] = jnp.zeros_like(state_ref)

    for idx, src in enumerate((q_ref, k_ref, v_ref)):
        xbuf_ref[0:8, :] = halo_ref[idx]
        xbuf_ref[8:8 + c, :] = src[...]
        halo_ref[idx] = src[c - 8:c, :]
        w = cw_ref[idx]
        acc = xbuf_ref[8:8 + c, :] * w[GDN_CONV - 1:GDN_CONV, :]
        for tap in range(GDN_CONV - 1):
            off = 8 - (GDN_CONV - 1) + tap
            acc = acc + xbuf_ref[off:off + c, :] * w[tap:tap + 1, :]
        qkv_ref[idx] = _silu(acc)

    row = lax.broadcasted_iota(jnp.int32, (c, c), 0)
    col = lax.broadcasted_iota(jnp.int32, (c, c), 1)
    causal = row >= col
    strict = row > col
    eye_f = (row == col).astype(F32)
    sm = sm_ref[...]
    gam_cols = _dot(causal.astype(F32), sm, HI)
    gam_rows = _dot_tn(sm, (col >= row).astype(F32), HI)

    heads = range(GDN_HEADS)
    hsl = lambda h: slice(h * HEAD_DIM, (h + 1) * HEAD_DIM)
    q = jnp.stack([qkv_ref[0, :, hsl(h)] for h in heads])
    k = jnp.stack([qkv_ref[1, :, hsl(h)] for h in heads])
    v = jnp.stack([qkv_ref[2, :, hsl(h)] for h in heads])
    beta = jnp.stack([sm[:, h:h + 1] for h in heads])
    gam = jnp.stack([gam_cols[:, GDN_HEADS + h:GDN_HEADS + h + 1] for h in heads])
    gam_r = jnp.stack([gam_rows[GDN_HEADS + h:GDN_HEADS + h + 1, :] for h in heads])
    q = q * lax.rsqrt(jnp.sum(q * q, axis=-1, keepdims=True) + EPS) * ATT_SCALE
    k = k * lax.rsqrt(jnp.sum(k * k, axis=-1, keepdims=True) + EPS)
    gam_last = gam[:, c - 1:c, :]
    eg = jnp.exp(gam)
    decay = jnp.where(causal, jnp.exp(jnp.where(causal, gam - gam_r, 0.0)), 0.0)
    kb = k * beta
    a_mat = jnp.where(strict, _bdot_nt(kb, k) * decay, 0.0)
    pw = -a_mat
    inv = eye_f + pw
    for _ in range(5):
        pw = _bdot(pw, pw)
        inv = inv + _bdot(inv, pw)
    uw = _bdot(inv, jnp.concatenate([v * beta, kb * eg], axis=-1))
    u, w = uw[..., :HEAD_DIM], uw[..., HEAD_DIM:]
    qk = jnp.where(causal, _bdot_nt(q, k) * decay, 0.0)
    state = state_ref[...]
    v_new = u - _bdot(w, state)
    out = _bdot(q * eg, state) + _bdot(qk, v_new)
    state_ref[...] = state * jnp.exp(gam_last) + _bdot_tn(k * jnp.exp(gam_last - gam), v_new)
    out = _rms(out, gn_ref[...])
    for h in heads:
        o_ref[:, hsl(h)] = out[h] * _silu(z_ref[:, hsl(h)])


def _gdn(proj, small, conv_w, out_norm):
    b, s, _ = proj.shape
    c = GDN_CHUNK
    w_h = GDN_HEADS * HEAD_DIM
    blk = lambda j: pl.BlockSpec((None, c, w_h), lambda bi, i: (bi, i, j))
    return pl.pallas_call(
        _gdn_kernel,
        grid=(b, s // c),
        in_specs=[blk(0), blk(1), blk(2), blk(3),
                  pl.BlockSpec((None, c, LANES), lambda bi, i: (bi, i, 0)),
                  pl.BlockSpec((3, GDN_CONV, w_h), lambda bi, i: (0, 0, 0)),
                  pl.BlockSpec((1, HEAD_DIM), lambda bi, i: (0, 0))],
        out_specs=pl.BlockSpec((None, c, w_h), lambda bi, i: (bi, i, 0)),
        out_shape=jax.ShapeDtypeStruct((b, s, w_h), F32),
        scratch_shapes=[pltpu.VMEM((3, 8, w_h), F32), pltpu.VMEM((c + 8, w_h), F32),
                        pltpu.VMEM((3, c, w_h), F32), pltpu.VMEM((GDN_HEADS, HEAD_DIM, HEAD_DIM), F32)],
        compiler_params=_cparams("parallel", "arbitrary"),
        name="gdn",
    )(proj, proj, proj, proj, small, conv_w, out_norm)


def _dsw_kernel(q_ref, kp_ref, kc_ref, vp_ref, vc_ref, o_ref, acc_ref, m_ref, l_ref):
    span = DSW_SPAN
    tile = DSW_TILE
    not_first = pl.program_id(2) > 0
    ri = lax.broadcasted_iota(jnp.int32, (span, 2 * span), 0)
    ci = lax.broadcasted_iota(jnp.int32, (span, 2 * span), 1)
    band = (ci >= ri) & (ci <= ri + span)
    band_first = band & ((ci >= span) | not_first)

    ones = jnp.ones((2 * span, HEAD_DIM), BF16)

    def unit(q, k, v, mask, rows, init):
        s = _dot_nt(q.astype(BF16), k.astype(BF16))
        s = jnp.where(mask, s, NEG_INF)
        mx = jnp.max(s, axis=-1, keepdims=True)
        va = jnp.concatenate([v.astype(BF16), ones], axis=-1)
        if init:
            m_ref[rows, :] = jnp.broadcast_to(mx, (span, LANES))
            pv = _dot(jnp.exp2(s - mx).astype(BF16), va)
            acc_ref[rows, :] = pv[:, :HEAD_DIM]
            l_ref[rows, :] = pv[:, HEAD_DIM:]
        else:
            m_old = m_ref[rows, :]
            m_new = jnp.maximum(m_old, mx)
            alpha = jnp.exp2(m_old - m_new)
            pv = _dot(jnp.exp2(s - pltpu.repeat(m_new, 2, axis=1)).astype(BF16), va)
            m_ref[rows, :] = m_new
            acc_ref[rows, :] = alpha * acc_ref[rows, :] + pv[:, :HEAD_DIM]
            l_ref[rows, :] = alpha * l_ref[rows, :] + pv[:, HEAD_DIM:]

    for bi, (window, dil) in enumerate(DSW_PAIRS):
        assert window // dil == span
        n_sub = tile // (span * dil)
        prev0 = tile - span * dil
        for r in range(dil):
            def rows_of(start, n=span):
                return pl.ds(start, n, stride=dil) if dil > 1 else pl.ds(start, n)
            k = jnp.concatenate([kp_ref[rows_of(prev0 + r), :], kc_ref[rows_of(r), :]], axis=0)
            v = jnp.concatenate([vp_ref[rows_of(prev0 + r), :], vc_ref[rows_of(r), :]], axis=0)
            unit(q_ref[rows_of(r), :], k, v, band_first, rows_of(r), bi == 0)
            for i in range(1, n_sub):
                st = r + span * dil * i
                unit(q_ref[rows_of(st), :], kc_ref[rows_of(st - span * dil, 2 * span), :],
                     vc_ref[rows_of(st - span * dil, 2 * span), :], band, rows_of(st), bi == 0)
    o_ref[...] = acc_ref[...] / l_ref[...]


def _dsw(qk, proj, v_col):
    b, s, _ = qk.shape
    tile = DSW_TILE
    cur = lambda off: pl.BlockSpec((None, tile, HEAD_DIM), lambda bi, h, t: (bi, t, off + h))
    prev = lambda off: pl.BlockSpec((None, tile, HEAD_DIM), lambda bi, h, t: (bi, jnp.maximum(t - 1, 0), off + h))
    return pl.pallas_call(
        _dsw_kernel,
        grid=(b, DSW_HEADS, s // tile),
        in_specs=[cur(0), prev(DSW_HEADS), cur(DSW_HEADS), prev(v_col), cur(v_col)],
        out_specs=cur(0),
        out_shape=jax.ShapeDtypeStruct((b, s, DSW_HEADS * HEAD_DIM), F32),
        scratch_shapes=[pltpu.VMEM((tile, HEAD_DIM), F32)] * 3,
        compiler_params=_cparams("parallel", "parallel", "arbitrary"),
        name="dsw",
    )(qk, qk, qk, proj, proj)


def _nsa_compress_kernel(x_ref, pos_ref, w1_ref, w2_ref, o_ref, shift_ref):
    st = NSA_CMP_STRIDE
    n = x_ref.shape[0]
    p_acc = None
    q_acc = None
    for l in range(st):
        xl = x_ref[:, l * HEAD_DIM:(l + 1) * HEAD_DIM].astype(F32)
        pt = _dot((xl + pos_ref[l:l + 1, :]).astype(BF16), w1_ref[l])
        qt = _dot((xl + pos_ref[st + l:st + l + 1, :]).astype(BF16), w1_ref[st + l])
        p_acc = pt if p_acc is None else p_acc + pt
        q_acc = qt if q_acc is None else q_acc + qt
    shift_ref[0:n, :] = q_acc
    shift_ref[n:n + 8, :] = jnp.zeros((8, q_acc.shape[1]), F32)
    hid = p_acc + shift_ref[pl.ds(1, n), :]
    o_ref[...] = _dot(_silu(hid).astype(BF16), w2_ref[...]).astype(o_ref.dtype)


def _nsa_compress(src, pos, w1, w2):
    b, s, _ = src.shape
    st = NSA_CMP_STRIDE
    n = s // st
    hid = w1.shape[-1]
    x16 = src.reshape(b, n, st, NSA_KV_HEADS, HEAD_DIM).transpose(0, 3, 1, 2, 4).reshape(b, NSA_KV_HEADS, n, st * HEAD_DIM)
    return pl.pallas_call(
        _nsa_compress_kernel,
        grid=(b, NSA_KV_HEADS),
        in_specs=[pl.BlockSpec((None, None, n, st * HEAD_DIM), lambda bi, g: (bi, g, 0, 0)),
                  pl.BlockSpec((NSA_CMP_BLOCK, HEAD_DIM), lambda bi, g: (0, 0)),
                  pl.BlockSpec((NSA_CMP_BLOCK, HEAD_DIM, hid), lambda bi, g: (0, 0, 0)),
                  pl.BlockSpec((hid, HEAD_DIM), lambda bi, g: (0, 0))],
        out_specs=pl.BlockSpec((None, None, n, HEAD_DIM), lambda bi, g: (bi, g, 0, 0)),
        out_shape=jax.ShapeDtypeStruct((b, NSA_KV_HEADS, n, HEAD_DIM), BF16),
        scratch_shapes=[pltpu.VMEM((n + 8, hid), F32)],
        compiler_params=_cparams("parallel", "parallel"),
        name="nsa_compress",
    )(x16, pos, w1, w2)


NSA_ROWS = 64


def _stack_heads(q_ref, dst_ref, lanes=slice(0, HEAD_DIM)):
    tq = q_ref.shape[0]
    for h in range(NSA_HPG):
        dst_ref[h * tq:(h + 1) * tq, lanes] = q_ref[:, h * HEAD_DIM:(h + 1) * HEAD_DIM]


def _unstack_heads(src, o_ref):
    tq = o_ref.shape[0]
    for h in range(NSA_HPG):
        o_ref[:, h * HEAD_DIM:(h + 1) * HEAD_DIM] = src[h * tq:(h + 1) * tq].astype(o_ref.dtype)


def _nsa_cmp_kernel(q_ref, kc_ref, vc_ref, ovt_ref, oc_ref, sel_ref, qs_ref, s_ref, p_ref, psum_ref):
    tq = q_ref.shape[0]
    ncp = kc_ref.shape[0]
    rb = NSA_ROWS
    t0 = pl.program_id(2) * tq
    _stack_heads(q_ref, qs_ref)
    s_ref[...] = _dot_nt(qs_ref[...], kc_ref[...])
    for rc in range(tq // rb):
        tpos = t0 + rc * rb + lax.broadcasted_iota(jnp.int32, (rb, ncp), 0)
        cend = lax.broadcasted_iota(jnp.int32, (rb, ncp), 1) * NSA_CMP_STRIDE + (NSA_CMP_BLOCK - 1)
        valid = cend <= tpos
        psum = None
        for h in range(NSA_HPG):
            rows = pl.ds(h * tq + rc * rb, rb)
            s = jnp.where(valid, s_ref[rows, :], NEG_INF)
            p = jnp.where(valid, jnp.exp2(s - jnp.max(s, axis=-1, keepdims=True)), 0.0)
            p = p * (1.0 / jnp.maximum(jnp.sum(p, axis=-1, keepdims=True), 1e-30))
            p_ref[rows, :] = p.astype(BF16)
            psum = p if psum is None else psum + p
        psum_ref[rc * rb:(rc + 1) * rb, :] = psum
    _unstack_heads(_dot(p_ref[...], vc_ref[...]), oc_ref)

    psum = psum_ref[...]
    p_hi = psum.astype(BF16)
    p_lo = (psum - p_hi.astype(F32)).astype(BF16)
    imp_t = _dot_nt(ovt_ref[...], p_hi) + _dot_nt(ovt_ref[...], p_lo)
    for cg in range(tq // LANES):
        jb = lax.broadcasted_iota(jnp.int32, (LANES, LANES), 0)
        tpos = t0 + cg * LANES + lax.broadcasted_iota(jnp.int32, (LANES, LANES), 1)
        cur = tpos // NSA_SLC_BLOCK
        forced = (jb == 0) | (jb == cur) | (jb == cur - 1)
        score = imp_t[:, cg * LANES:(cg + 1) * LANES] + jnp.where(forced, NSA_FORCE_BONUS, 0.0)
        score = jnp.where(jb <= cur, score, NEG_INF)
        sel = jnp.full((LANES, LANES), NEG_INF, F32)
        for _ in range(NSA_SLC_TOPK):
            top = jnp.max(score, axis=0, keepdims=True)
            idx = jnp.min(jnp.where(score == top, jb, LANES), axis=0, keepdims=True)
            hit = jb == idx
            sel = jnp.where(hit, 0.0, sel)
            score = jnp.where(hit, -3e38, score)
        sel_ref[cg * LANES:(cg + 1) * LANES, :] = sel.T.astype(BF16)


NSA_KEY_TILE = 512


def _nsa_slc_kernel(q_ref, sel_ref, k_ref, v_ref, oh_ref, o_ref, qa_ref, s_ref, p_ref, acc_ref, m_ref):
    tq = q_ref.shape[0]
    tk = NSA_KEY_TILE
    rb = NSA_ROWS
    t0 = pl.program_id(2) * tq
    _stack_heads(q_ref, qa_ref)
    for h in range(NSA_HPG):
        qa_ref[h * tq:(h + 1) * tq, HEAD_DIM:] = sel_ref[...]
    acc_ref[...] = jnp.zeros_like(acc_ref)
    m_ref[...] = jnp.full(m_ref.shape, NEG_INF, F32)
    ones = jnp.ones((tk, HEAD_DIM), BF16)

    def tile(jt, on_diagonal):
        k0 = pl.multiple_of(jt * tk, tk)
        ka = jnp.concatenate([k_ref[pl.ds(k0, tk), :], oh_ref[pl.ds(k0, tk), :]], axis=-1)
        va = jnp.concatenate([v_ref[pl.ds(k0, tk), :], ones], axis=-1)
        for h in range(NSA_HPG):
            s_ref[h] = _dot_nt(qa_ref[h * tq:(h + 1) * tq, :], ka)
            for rc in range(tq // rb):
                rows = pl.ds(rc * rb, rb)
                s = s_ref[h, rows, :]
                if on_diagonal:
                    tpos = t0 + rc * rb + lax.broadcasted_iota(jnp.int32, (rb, tk), 0)
                    s = jnp.where(k0 + lax.broadcasted_iota(jnp.int32, (rb, tk), 1) <= tpos, s, NEG_INF)
                m_old = m_ref[h, rows, :]
                m_new = jnp.maximum(m_old, jnp.max(s, axis=-1, keepdims=True))
                p = jnp.exp2(s - pltpu.repeat(m_new, tk // LANES, axis=1))
                m_ref[h, rows, :] = m_new
                acc_ref[h, rows, :] = pltpu.repeat(jnp.exp2(m_old - m_new), 2, axis=1) * acc_ref[h, rows, :]
                p_ref[h, rows, :] = p.astype(BF16)
            acc_ref[h] += _dot(p_ref[h], va)

    n_below = t0 // tk

    def pair(jp, carry):
        tile(2 * jp, False)
        tile(2 * jp + 1, False)
        return carry

    lax.fori_loop(0, n_below // 2, pair, 0)

    @pl.when(n_below % 2 == 1)
    def _():
        tile(n_below - 1, False)

    tile(n_below, True)
    for h in range(NSA_HPG):
        o_ref[:, h * HEAD_DIM:(h + 1) * HEAD_DIM] = (acc_ref[h, :, :HEAD_DIM] / acc_ref[h, :, HEAD_DIM:]).astype(o_ref.dtype)


def _nsa_win_kernel(q_ref, k_ref, v_ref, o_ref, qs_ref, s_ref, p_ref):
    tq = q_ref.shape[0]
    nk = NSA_WINDOW + tq
    rb = NSA_ROWS
    t0 = pl.program_id(2) * tq
    start = pl.multiple_of(jnp.maximum(t0 - NSA_WINDOW, 0), tq)
    _stack_heads(q_ref, qs_ref)
    s_ref[...] = _dot_nt(qs_ref[...], k_ref[pl.ds(start, nk), :])
    for rc in range(tq // rb):
        tpos = t0 + rc * rb + lax.broadcasted_iota(jnp.int32, (rb, nk), 0)
        kpos = start + lax.broadcasted_iota(jnp.int32, (rb, nk), 1)
        valid = (kpos <= tpos) & (kpos > tpos - NSA_WINDOW)
        for h in range(NSA_HPG):
            rows = pl.ds(h * tq + rc * rb, rb)
            s = jnp.where(valid, s_ref[rows, :], NEG_INF)
            p = jnp.exp2(s - jnp.max(s, axis=-1, keepdims=True))
            p_ref[rows, :] = (p * (1.0 / jnp.sum(p, axis=-1, keepdims=True))).astype(BF16)
    _unstack_heads(_dot(p_ref[...], v_ref[pl.ds(start, nk), :]), o_ref)


def _nsa_overlap_t(n_cmp_pad, n_cmp):
    c_start = np.arange(n_cmp_pad)[None, :] * NSA_CMP_STRIDE
    s_start = np.arange(LANES)[:, None] * NSA_SLC_BLOCK
    ov = (c_start < s_start + NSA_SLC_BLOCK) & (c_start + NSA_CMP_BLOCK > s_start)
    ov &= (np.arange(n_cmp_pad) < n_cmp)[None, :]
    return jnp.asarray(ov, BF16)


def _nsa_attention(hd, pv, kc, vc, tq_cmp=512, tq_slc=512, tq_win=256):
    b, s, _ = hd.shape
    g = NSA_KV_HEADS
    qw = NSA_HPG * HEAD_DIM
    ncp = s // NSA_CMP_STRIDE
    nk_win = NSA_WINDOW + tq_win
    assert s // NSA_SLC_BLOCK <= LANES and s >= nk_win and NSA_KEY_TILE % tq_slc == 0
    grid = lambda tq: (b, g, s // tq)
    q_spec = lambda tq: pl.BlockSpec((None, tq, qw), lambda bi, gi, i: (bi, i, gi))
    seq_spec = lambda off: pl.BlockSpec((None, s, HEAD_DIM), lambda bi, gi, i: (bi, 0, off + gi))
    cmp_spec = pl.BlockSpec((None, None, ncp, HEAD_DIM), lambda bi, gi, i: (bi, gi, 0, 0))
    sel_spec = lambda tq: pl.BlockSpec((None, None, tq, LANES), lambda bi, gi, i: (bi, gi, i, 0))
    const_spec = lambda shape: pl.BlockSpec(shape, lambda bi, gi, i: (0, 0))
    out_sd = jax.ShapeDtypeStruct((b, s, NSA_HEADS * HEAD_DIM), BF16)
    sel_sd = jax.ShapeDtypeStruct((b, g, s, LANES), BF16)
    sem = ("parallel", "parallel", "arbitrary")
    k_slc, k_win = NSA_HEADS + g, NSA_HEADS + 2 * g
    v_slc, v_win = g, 2 * g
    block_onehot = jnp.asarray(np.arange(s)[:, None] // NSA_SLC_BLOCK == np.arange(LANES)[None, :], BF16)

    rows = NSA_HPG * tq_cmp
    o_c, sel = pl.pallas_call(
        _nsa_cmp_kernel, grid=grid(tq_cmp),
        in_specs=[q_spec(tq_cmp), cmp_spec, cmp_spec, const_spec((LANES, ncp))],
        out_specs=[q_spec(tq_cmp), sel_spec(tq_cmp)], out_shape=[out_sd, sel_sd],
        scratch_shapes=[pltpu.VMEM((rows, HEAD_DIM), BF16), pltpu.VMEM((rows, ncp), F32),
                        pltpu.VMEM((rows, ncp), BF16), pltpu.VMEM((tq_cmp, ncp), F32)],
        compiler_params=_cparams(*sem), name="nsa_cmp",
    )(hd, kc, vc, _nsa_overlap_t(ncp, ncp - 1))
    tq = tq_slc
    o_s = pl.pallas_call(
        _nsa_slc_kernel, grid=grid(tq),
        in_specs=[q_spec(tq), sel_spec(tq), seq_spec(k_slc), seq_spec(v_slc), const_spec((s, LANES))],
        out_specs=q_spec(tq), out_shape=out_sd,
        scratch_shapes=[pltpu.VMEM((NSA_HPG * tq, 2 * HEAD_DIM), BF16), pltpu.VMEM((NSA_HPG, tq, NSA_KEY_TILE), F32),
                        pltpu.VMEM((NSA_HPG, tq, NSA_KEY_TILE), BF16), pltpu.VMEM((NSA_HPG, tq, 2 * HEAD_DIM), F32),
                        pltpu.VMEM((NSA_HPG, tq, HEAD_DIM), F32)],
        compiler_params=_cparams(*sem), name="nsa_slc",
    )(hd, sel, hd, pv, block_onehot)
    rows = NSA_HPG * tq_win
    o_w = pl.pallas_call(
        _nsa_win_kernel, grid=grid(tq_win),
        in_specs=[q_spec(tq_win), seq_spec(k_win), seq_spec(v_win)],
        out_specs=q_spec(tq_win), out_shape=out_sd,
        scratch_shapes=[pltpu.VMEM((rows, HEAD_DIM), BF16), pltpu.VMEM((rows, nk_win), F32),
                        pltpu.VMEM((rows, nk_win), BF16)],
        compiler_params=_cparams(*sem), name="nsa_win",
    )(hd, hd, pv)
    return o_c, o_s, o_w


def _row(v, width=None):
    v = v.reshape(1, -1).astype(F32)
    if width is not None and v.shape[1] < width:
        v = jnp.pad(v, ((0, 0), (0, width - v.shape[1])))
    return v


def _even_mixer(xf, b, s, gain, w_in, w_out, conv_w, a_log, dt_bias, gdn_norm, q_norm, k_norm, rope):
    gw = GDN_HEADS * HEAD_DIM
    dw = DSW_HEADS * HEAD_DIM
    o_small = 4 * gw
    o_dsw = o_small + 2 * GDN_HEADS
    w_plain = jnp.concatenate([w_in[:, :o_small], w_in[:, o_dsw + 2 * dw:]], axis=1).astype(BF16)
    w_heads = w_in[:, o_dsw:o_dsw + 2 * dw].astype(BF16)
    w_small = jnp.pad(w_in[:, o_small:o_dsw], ((0, 0), (0, LANES - 2 * GDN_HEADS))).astype(BF16)
    pad8 = lambda v: jnp.pad(_row(v), ((0, 0), (GDN_HEADS, LANES - 2 * GDN_HEADS)))
    head_gain = jnp.concatenate([jnp.tile(_row(q_norm) * Q_SCALE_LOG2, (1, DSW_HEADS)),
                                 jnp.tile(_row(k_norm), (1, DSW_HEADS))], axis=1)

    proj = _norm_proj(xf, gain, w_plain, "plain")
    small = _norm_proj(xf, gain, w_small, "gdn_gates", (pad8(a_log), pad8(dt_bias)))
    qk = _norm_proj(xf, gain, w_heads, "heads", (head_gain, *rope))
    cw = conv_w.reshape(GDN_CONV, 3, gw).transpose(1, 0, 2)
    o_a = _gdn(proj.reshape(b, s, -1), small.reshape(b, s, LANES), cw, _row(gdn_norm))
    o_b = _dsw(qk.reshape(b, s, -1), proj.reshape(b, s, -1), o_small // HEAD_DIM)
    t = b * s
    return _proj_res(xf, [o_a.reshape(t, gw), o_b.reshape(t, dw)],
                     [w_out[:gw].astype(BF16), w_out[gw:].astype(BF16)])


def _odd_mixer(xf, b, s, gain, w_in, w_out, q_norm, k_norm, cmp_pos, cmp_w1, cmp_w2, rope):
    qw = NSA_HEADS * HEAD_DIM
    kvw = NSA_KV_HEADS * HEAD_DIM
    col = lambda i: slice(qw + i * kvw, qw + (i + 1) * kvw)
    w_heads = jnp.concatenate([w_in[:, :qw], w_in[:, col(0)], w_in[:, col(2)], w_in[:, col(4)]], axis=1).astype(BF16)
    w_plain = jnp.concatenate([w_in[:, col(1)], w_in[:, col(3)], w_in[:, col(5)]], axis=1).astype(BF16)
    w_gate = jnp.pad(w_in[:, qw + 6 * kvw:], ((0, 0), (0, LANES - 3 * NSA_HEADS))).astype(BF16)
    head_gain = jnp.concatenate([jnp.tile(_row(q_norm) * Q_SCALE_LOG2, (1, NSA_HEADS))]
                                + [jnp.tile(_row(k_norm[i]), (1, NSA_KV_HEADS)) for i in range(3)], axis=1)

    hd = _norm_proj(xf, gain, w_heads, "heads", (head_gain, *rope), out_dtype=BF16).reshape(b, s, -1)
    pv = _norm_proj(xf, gain, w_plain, "plain", out_dtype=BF16).reshape(b, s, -1)
    gates = _norm_proj(xf, gain, w_gate, "sigmoid")
    hid = cmp_w1.shape[-1]
    w1 = cmp_w1.reshape(2, NSA_CMP_BLOCK, HEAD_DIM, hid).astype(BF16)
    kc = _nsa_compress(hd[:, :, qw:qw + kvw], cmp_pos[0], w1[0], cmp_w2[0].astype(BF16))
    vc = _nsa_compress(pv[:, :, :kvw], cmp_pos[1], w1[1], cmp_w2[1].astype(BF16))
    o_c, o_s, o_w = _nsa_attention(hd, pv, kc, vc)
    t = b * s
    return _proj_res(xf, [o_c.reshape(t, qw), o_s.reshape(t, qw), o_w.reshape(t, qw)], [w_out.astype(BF16)], gates)


def kernel(x, mem, positions, ffn1_norm, ffn1_w_gu, ffn1_w_down, mix_norm, ev_w_in, ev_w_out, gdn_conv_w, gdn_a_log, gdn_dt_bias, gdn_out_norm, dsw_q_norm, dsw_k_norm, od_w_in, od_w_out, nsa_q_norm, nsa_k_norm, nsa_cmp_pos, nsa_cmp_w1, nsa_cmp_w2, xa_norm, xa_mem_norm, xa_w_q, xa_w_kv, xa_q_norm, xa_k_norm, xa_w_o, ffn2_norm, ffn2_w_gu, ffn2_w_down):
    b, s, d = x.shape
    depth = ffn1_norm.shape[0]
    rope = _rope_tables(positions)
    xf = x.reshape(b * s, d)
    for i in range(depth):
        xf = _ffn(xf, _row(ffn1_norm[i]), ffn1_w_gu[i].astype(BF16), ffn1_w_down[i].astype(BF16))
        if i % 2 == 0:
            e = i // 2
            xf = _even_mixer(xf, b, s, _row(mix_norm[i]), ev_w_in[e], ev_w_out[e], gdn_conv_w[e], gdn_a_log[e],
                             gdn_dt_bias[e], gdn_out_norm[e], dsw_q_norm[e], dsw_k_norm[e], rope)
        else:
            o = i // 2
            xf = _odd_mixer(xf, b, s, _row(mix_norm[i]), od_w_in[o], od_w_out[o], nsa_q_norm[o], nsa_k_norm[o],
                            nsa_cmp_pos[o], nsa_cmp_w1[o], nsa_cmp_w2[o], rope)
        xf = _cross_attention(xf.reshape(b, s, d), mem, _row(xa_norm[i]), _row(xa_mem_norm[i]),
                              xa_w_q[i].astype(BF16), xa_w_kv[i].astype(BF16), _row(xa_q_norm[i]),
                              _row(xa_k_norm[i]), xa_w_o[i].astype(BF16)).reshape(b * s, d)
        xf = _ffn(xf, _row(ffn2_norm[i]), ffn2_w_gu[i].astype(BF16), ffn2_w_down[i].astype(BF16))
    return xf.reshape(b, s, d)
```

```python
import functools

import numpy as np
import jax
import jax.numpy as jnp
from jax import lax
from jax.experimental import pallas as pl
from jax.experimental.pallas import tpu as pltpu

F32 = jnp.float32
BF16 = jnp.bfloat16
HI = lax.Precision.HIGHEST

EPS = 1e-6
NEG_INF = -1e30
HEAD_DIM = 128
ROPE_THETA = 500000.0
ROPE_DIM = HEAD_DIM // 4
ATT_SCALE = HEAD_DIM ** -0.5
LOG2E = float(np.log2(np.e))
Q_SCALE_LOG2 = ATT_SCALE * LOG2E

GDN_HEADS = 8
GDN_CHUNK = 64
GDN_CONV = 4
DSW_HEADS = 8
DSW_PAIRS = ((128, 1), (512, 4), (2048, 16))
DSW_SPAN = 128
DSW_TILE = 2048
NSA_HEADS = 16
NSA_KV_HEADS = 4
NSA_HPG = NSA_HEADS // NSA_KV_HEADS
NSA_CMP_BLOCK = 32
NSA_CMP_STRIDE = 16
NSA_SLC_BLOCK = 64
NSA_SLC_TOPK = 16
NSA_WINDOW = 512
NSA_FORCE_BONUS = 1e3
XA_HEADS = 4

VMEM_LIMIT_BYTES = 56 * 1024 * 1024
LANES = 128


def _cparams(*sem):
    return pltpu.CompilerParams(dimension_semantics=sem, vmem_limit_bytes=VMEM_LIMIT_BYTES)


def _dot(a, b, precision=None):
    return jnp.dot(a, b, preferred_element_type=F32, precision=precision)


def _dot_nt(a, b, precision=None):
    return lax.dot_general(a, b, (((1,), (1,)), ((), ())), preferred_element_type=F32, precision=precision)


def _dot_tn(a, b, precision=None):
    return lax.dot_general(a, b, (((0,), (0,)), ((), ())), preferred_element_type=F32, precision=precision)


def _bmm(a, b, dn):
    return lax.dot_general(a.astype(BF16), b.astype(BF16), dn, preferred_element_type=F32)


def _bdot(a, b):
    return _bmm(a, b, (((2,), (1,)), ((0,), (0,))))


def _bdot_nt(a, b):
    return _bmm(a, b, (((2,), (2,)), ((0,), (0,))))


def _bdot_tn(a, b):
    return _bmm(a, b, (((1,), (1,)), ((0,), (0,))))


def _rms(x, gain):
    return x * lax.rsqrt(jnp.mean(x * x, axis=-1, keepdims=True) + EPS) * gain


def _silu(x):
    return x * jax.nn.sigmoid(x)


def _lane_tile(x, n):
    return jnp.concatenate([x] * n, axis=1)


def _ffn_kernel(x_ref, g_ref, wg_ref, wu_ref, wd_ref, o_ref, xn_ref, acc_ref):
    f = pl.program_id(1)

    @pl.when(f == 0)
    def _():
        xn_ref[...] = _rms(x_ref[...], g_ref[...]).astype(BF16)
        acc_ref[...] = jnp.zeros_like(acc_ref)

    xn = xn_ref[...]
    gate = _dot(xn, wg_ref[...])
    up = _dot(xn, wu_ref[...])
    act = (_silu(gate) * up).astype(BF16)
    acc_ref[...] += _dot(act, wd_ref[...])

    @pl.when(f == pl.num_programs(1) - 1)
    def _():
        o_ref[...] = x_ref[...] + 0.5 * acc_ref[...]


def _ffn(x, gain, w_gu, w_down, tm=512, tf=512):
    t, d = x.shape
    d_ff = w_down.shape[0]
    n_f = d_ff // tf
    return pl.pallas_call(
        _ffn_kernel,
        grid=(t // tm, n_f),
        in_specs=[
            pl.BlockSpec((tm, d), lambda i, f: (i, 0)),
            pl.BlockSpec((1, d), lambda i, f: (0, 0)),
            pl.BlockSpec((d, tf), lambda i, f: (0, f)),
            pl.BlockSpec((d, tf), lambda i, f: (0, f + n_f)),
            pl.BlockSpec((tf, d), lambda i, f: (f, 0)),
        ],
        out_specs=pl.BlockSpec((tm, d), lambda i, f: (i, 0)),
        out_shape=jax.ShapeDtypeStruct((t, d), F32),
        scratch_shapes=[pltpu.VMEM((tm, d), BF16), pltpu.VMEM((tm, d), F32)],
        compiler_params=_cparams("parallel", "arbitrary"),
        name="ffn",
    )(x, gain, w_gu, w_gu, w_down)


ROPE_HALF = ROPE_DIM // 2
ROPE_SHIFT = LANES // 2
ROPE_PERM = np.arange(HEAD_DIM)
ROPE_PERM[ROPE_HALF:ROPE_DIM] = np.arange(ROPE_SHIFT, ROPE_SHIFT + ROPE_HALF)
ROPE_PERM[ROPE_SHIFT:ROPE_SHIFT + ROPE_HALF] = np.arange(ROPE_HALF, ROPE_DIM)


def _perm_head_dims(a, axis=-1):
    n_heads = a.shape[axis] // HEAD_DIM
    idx = (np.arange(n_heads)[:, None] * HEAD_DIM + ROPE_PERM[None, :]).reshape(-1)
    return jnp.take(a, idx, axis=axis)


def _rope_kernel(pos_ref, invf_ref, c_ref, s_ref):
    ang = pos_ref[...] * invf_ref[...]
    lane = lax.broadcasted_iota(jnp.int32, ang.shape, 1)
    sin = jnp.sin(ang)
    c_ref[...] = jnp.cos(ang)
    s_ref[...] = jnp.where(lane < ROPE_SHIFT, -sin, sin)


def _rope_tables(positions, tm=1024):
    t = positions.size
    inv_freq = jnp.float32(ROPE_THETA) ** (-jnp.arange(ROPE_HALF, dtype=F32) / ROPE_HALF)
    invf = jnp.zeros((1, LANES), F32).at[0, :ROPE_HALF].set(inv_freq).at[0, ROPE_SHIFT:ROPE_SHIFT + ROPE_HALF].set(inv_freq)
    pos = positions.astype(F32).reshape(t, 1)
    spec = pl.BlockSpec((tm, LANES), lambda i: (i, 0))
    return pl.pallas_call(
        _rope_kernel,
        grid=(t // tm,),
        in_specs=[pl.BlockSpec((tm, 1), lambda i: (i, 0)), pl.BlockSpec((1, LANES), lambda i: (0, 0))],
        out_specs=[spec, spec],
        out_shape=[jax.ShapeDtypeStruct((t, LANES), F32)] * 2,
        compiler_params=_cparams("parallel"),
        name="rope_tables",
    )(pos, invf)


def _rope(x, c, s):
    return x * c + pltpu.roll(x, ROPE_SHIFT, 1) * s


def _norm_proj_kernel(mode, x_ref, g_ref, w_ref, *rest):
    o_ref, xn_ref = rest[-2], rest[-1]

    @pl.when(pl.program_id(1) == 0)
    def _():
        xn_ref[...] = _rms(x_ref[...], g_ref[...]).astype(BF16)

    y = _dot(xn_ref[...], w_ref[...])
    if mode == "plain":
        o_ref[...] = y.astype(o_ref.dtype)
    elif mode == "sigmoid":
        o_ref[...] = jax.nn.sigmoid(y)
    elif mode == "gdn_gates":
        alog_ref, dt_ref = rest[0], rest[1]
        lane = lax.broadcasted_iota(jnp.int32, y.shape, 1)
        z = y + dt_ref[...]
        softplus = jnp.maximum(z, 0.0) + jnp.log1p(jnp.exp(-jnp.abs(z)))
        o_ref[...] = jnp.where(lane < GDN_HEADS, jax.nn.sigmoid(y), -jnp.exp(alog_ref[...]) * softplus)
    else:
        raise ValueError(mode)


def _norm_proj(x, gain, w, mode, extras=(), out_dtype=F32, tm=512, tn=512):
    t, d = x.shape
    n = w.shape[1]
    tn = min(tn, n)
    assert t % tm == 0 and n % tn == 0
    in_specs = [
        pl.BlockSpec((tm, d), lambda i, j: (i, 0)),
        pl.BlockSpec((1, d), lambda i, j: (0, 0)),
        pl.BlockSpec((d, tn), lambda i, j: (0, j)),
    ]
    if mode == "gdn_gates":
        in_specs += [pl.BlockSpec((1, tn), lambda i, j: (0, j))] * 2
    return pl.pallas_call(
        functools.partial(_norm_proj_kernel, mode),
        grid=(t // tm, n // tn),
        in_specs=in_specs,
        out_specs=pl.BlockSpec((tm, tn), lambda i, j: (i, j)),
        out_shape=jax.ShapeDtypeStruct((t, n), out_dtype),
        scratch_shapes=[pltpu.VMEM((tm, d), BF16)],
        compiler_params=_cparams("parallel", "arbitrary"),
        name="norm_proj_" + mode,
    )(x, gain, w, *extras)


def _norm_proj_heads_kernel(x_ref, g_ref, w_ref, hg_ref, c_ref, s_ref, o_ref):
    xn = _rms(x_ref[...], g_ref[...]).astype(BF16)
    c, s = c_ref[...], s_ref[...]
    pair = 2 * HEAD_DIM
    for kp in range(w_ref.shape[1] // pair):
        y = _dot(xn, w_ref[:, kp * pair:(kp + 1) * pair])
        for k in range(2):
            sl = slice(kp * pair + k * HEAD_DIM, kp * pair + (k + 1) * HEAD_DIM)
            yk = y[:, k * HEAD_DIM:(k + 1) * HEAD_DIM]
            o_ref[:, sl] = _rope(_rms(yk, hg_ref[:, sl]), c, s).astype(o_ref.dtype)


def _norm_proj_heads(x, gain, w, head_gain, rope, out_dtype, tm=512):
    t, d = x.shape
    n = w.shape[1]
    assert t % tm == 0 and n % (2 * HEAD_DIM) == 0
    w = _perm_head_dims(w)
    head_gain = _perm_head_dims(head_gain)
    full = lambda shape: pl.BlockSpec(shape, lambda i: (0, 0))
    row = lambda width: pl.BlockSpec((tm, width), lambda i: (i, 0))
    return pl.pallas_call(
        _norm_proj_heads_kernel,
        grid=(t // tm,),
        in_specs=[row(d), full((1, d)), full((d, n)), full((1, n)), row(LANES), row(LANES)],
        out_specs=row(n),
        out_shape=jax.ShapeDtypeStruct((t, n), out_dtype),
        compiler_params=_cparams("parallel"),
        name="norm_proj_heads",
    )(x, gain, w, head_gain, *rope)


def _proj_res_kernel(n_in, gated, *refs):
    x_ref = refs[0]
    a_refs = refs[1:1 + n_in]
    o_ref = refs[-1]
    w_refs = refs[1 + n_in:-2] if gated else refs[1 + n_in:-1]
    if gated:
        gt = refs[-2][...]
        cols = []
        for h in range(NSA_HEADS):
            sl = slice(h * HEAD_DIM, (h + 1) * HEAD_DIM)
            acc = None
            for j in range(n_in):
                term = gt[:, 3 * h + j:3 * h + j + 1] * a_refs[j][:, sl]
                acc = term if acc is None else acc + term
            cols.append(acc.astype(BF16))
        y = _dot(jnp.concatenate(cols, axis=-1), w_refs[0][...])
    else:
        y = None
        for a_ref, w_ref in zip(a_refs, w_refs):
            term = _dot(a_ref[...].astype(BF16), w_ref[...])
            y = term if y is None else y + term
    o_ref[...] = x_ref[...] + y


def _proj_res(x, acts, ws, gates=None, tm=512):
    t, d = x.shape
    n_in = len(acts)
    row = lambda i: (i, 0)
    in_specs = [pl.BlockSpec((tm, d), row)]
    in_specs += [pl.BlockSpec((tm, a.shape[1]), row) for a in acts]
    in_specs += [pl.BlockSpec(w.shape, lambda i: (0, 0)) for w in ws]
    args = [x, *acts, *ws]
    if gates is not None:
        in_specs.append(pl.BlockSpec((tm, LANES), row))
        args.append(gates)
    return pl.pallas_call(
        functools.partial(_proj_res_kernel, n_in, gates is not None),
        grid=(t // tm,),
        in_specs=in_specs,
        out_specs=pl.BlockSpec((tm, d), row),
        out_shape=jax.ShapeDtypeStruct((t, d), F32),
        compiler_params=_cparams("parallel"),
        name="proj_res",
    )(*args)


def _xa_kv_kernel(mem_ref, gm_ref, w_ref, kn_ref, k_ref, v_ref):
    kv = _dot(_rms(mem_ref[...], gm_ref[...]).astype(BF16), w_ref[...])
    xa_w = XA_HEADS * HEAD_DIM
    for h in range(XA_HEADS):
        sl = slice(h * HEAD_DIM, (h + 1) * HEAD_DIM)
        k_ref[:, sl] = _rms(kv[:, sl], kn_ref[...]).astype(BF16)
    v_ref[...] = kv[:, xa_w:].astype(BF16)


def _xa_kernel(x_ref, g_ref, wq_ref, qn_ref, k_ref, v_ref, wo_ref, o_ref):
    x = x_ref[...]
    q = _dot(_rms(x, g_ref[...]).astype(BF16), wq_ref[...])
    outs = []
    for h in range(XA_HEADS):
        sl = slice(h * HEAD_DIM, (h + 1) * HEAD_DIM)
        qh = (_rms(q[:, sl], qn_ref[...]) * Q_SCALE_LOG2).astype(BF16)
        s = _dot_nt(qh, k_ref[:, sl])
        p = jnp.exp2(s - jnp.max(s, axis=-1, keepdims=True))
        l = jnp.sum(p, axis=-1, keepdims=True)
        outs.append((_dot(p.astype(BF16), v_ref[:, sl]) / l).astype(BF16))
    o_ref[...] = x + _dot(jnp.concatenate(outs, axis=-1), wo_ref[...])


def _cross_attention(x, mem, g_x, g_mem, w_q, w_kv, q_norm, k_norm, w_o, tm=512):
    b, s, d = x.shape
    m_len = mem.shape[1]
    xa_w = XA_HEADS * HEAD_DIM
    full = lambda shape: pl.BlockSpec(shape, lambda *_: (0,) * len(shape))
    k, v = pl.pallas_call(
        _xa_kv_kernel,
        grid=(b,),
        in_specs=[pl.BlockSpec((None, m_len, d), lambda i: (i, 0, 0)), full((1, d)), full((d, 2 * xa_w)),
                  full((1, HEAD_DIM))],
        out_specs=[pl.BlockSpec((None, m_len, xa_w), lambda i: (i, 0, 0))] * 2,
        out_shape=[jax.ShapeDtypeStruct((b, m_len, xa_w), BF16)] * 2,
        compiler_params=_cparams("parallel"),
        name="xa_kv",
    )(mem, g_mem, w_kv, k_norm)
    return pl.pallas_call(
        _xa_kernel,
        grid=(b, s // tm),
        in_specs=[pl.BlockSpec((None, tm, d), lambda i, j: (i, j, 0)), full((1, d)), full((d, xa_w)),
                  full((1, HEAD_DIM)),
                  pl.BlockSpec((None, m_len, xa_w), lambda i, j: (i, 0, 0)),
                  pl.BlockSpec((None, m_len, xa_w), lambda i, j: (i, 0, 0)),
                  full((xa_w, d))],
        out_specs=pl.BlockSpec((None, tm, d), lambda i, j: (i, j, 0)),
        out_shape=jax.ShapeDtypeStruct((b, s, d), F32),
        compiler_params=_cparams("parallel", "parallel"),
        name="cross_attn",
    )(x, g_x, w_q, q_norm, k, v, w_o)


def _gdn_kernel(q_ref, k_ref, v_ref, z_ref, sm_ref, cw_ref, gn_ref, o_ref, halo_ref, xbuf_ref, qkv_ref, state_ref):
    c = GDN_CHUNK

    @pl.when(pl.program_id(1) == 0)
    def _():
        halo_ref[...] = jnp.zeros_like(halo_ref)
        state_ref[...] = jnp.zeros_like(state_ref)

    for idx, src in enumerate((q_ref, k_ref, v_ref)):
        xbuf_ref[0:8, :] = halo_ref[idx]
        xbuf_ref[8:8 + c, :] = src[...]
        halo_ref[idx] = src[c - 8:c, :]
        w = cw_ref[idx]
        acc = xbuf_ref[8:8 + c, :] * w[GDN_CONV - 1:GDN_CONV, :]
        for tap in range(GDN_CONV - 1):
            off = 8 - (GDN_CONV - 1) + tap
            acc = acc + xbuf_ref[off:off + c, :] * w[tap:tap + 1, :]
        qkv_ref[idx] = _silu(acc)

    row = lax.broadcasted_iota(jnp.int32, (c, c), 0)
    col = lax.broadcasted_iota(jnp.int32, (c, c), 1)
    causal = row >= col
    strict = row > col
    eye_f = (row == col).astype(F32)
    sm = sm_ref[...]
    gam_cols = _dot(causal.astype(F32), sm, HI)
    gam_rows = _dot_tn(sm, (col >= row).astype(F32), HI)

    heads = range(GDN_HEADS)
    hsl = lambda h: slice(h * HEAD_DIM, (h + 1) * HEAD_DIM)
    q = jnp.stack([qkv_ref[0, :, hsl(h)] for h in heads])
    k = jnp.stack([qkv_ref[1, :, hsl(h)] for h in heads])
    v = jnp.stack([qkv_ref[2, :, hsl(h)] for h in heads])
    beta = jnp.stack([sm[:, h:h + 1] for h in heads])
    gam = jnp.stack([gam_cols[:, GDN_HEADS + h:GDN_HEADS + h + 1] for h in heads])
    gam_r = jnp.stack([gam_rows[GDN_HEADS + h:GDN_HEADS + h + 1, :] for h in heads])
    q = q * lax.rsqrt(jnp.sum(q * q, axis=-1, keepdims=True) + EPS) * ATT_SCALE
    k = k * lax.rsqrt(jnp.sum(k * k, axis=-1, keepdims=True) + EPS)
    gam_last = gam[:, c - 1:c, :]
    eg = jnp.exp(gam)
    decay = jnp.where(causal, jnp.exp(jnp.where(causal, gam - gam_r, 0.0)), 0.0)
    kb = k * beta
    a_mat = jnp.where(strict, _bdot_nt(kb, k) * decay, 0.0)
    pw = -a_mat
    inv = eye_f + pw
    for _ in range(5):
        pw = _bdot(pw, pw)
        inv = inv + _bdot(inv, pw)
    uw = _bdot(inv, jnp.concatenate([v * beta, kb * eg], axis=-1))
    u, w = uw[..., :HEAD_DIM], uw[..., HEAD_DIM:]
    qk = jnp.where(causal, _bdot_nt(q, k) * decay, 0.0)
    state = state_ref[...]
    v_new = u - _bdot(w, state)
    out = _bdot(q * eg, state) + _bdot(qk, v_new)
    state_ref[...] = state * jnp.exp(gam_last) + _bdot_tn(k * jnp.exp(gam_last - gam), v_new)
    out = _rms(out, gn_ref[...])
    for h in heads:
        o_ref[:, hsl(h)] = out[h] * _silu(z_ref[:, hsl(h)])


def _gdn(proj, small, conv_w, out_norm):
    b, s, _ = proj.shape
    c = GDN_CHUNK
    w_h = GDN_HEADS * HEAD_DIM
    blk = lambda j: pl.BlockSpec((None, c, w_h), lambda bi, i: (bi, i, j))
    return pl.pallas_call(
        _gdn_kernel,
        grid=(b, s // c),
        in_specs=[blk(0), blk(1), blk(2), blk(3),
                  pl.BlockSpec((None, c, LANES), lambda bi, i: (bi, i, 0)),
                  pl.BlockSpec((3, GDN_CONV, w_h), lambda bi, i: (0, 0, 0)),
                  pl.BlockSpec((1, HEAD_DIM), lambda bi, i: (0, 0))],
        out_specs=pl.BlockSpec((None, c, w_h), lambda bi, i: (bi, i, 0)),
        out_shape=jax.ShapeDtypeStruct((b, s, w_h), F32),
        scratch_shapes=[pltpu.VMEM((3, 8, w_h), F32), pltpu.VMEM((c + 8, w_h), F32),
                        pltpu.VMEM((3, c, w_h), F32), pltpu.VMEM((GDN_HEADS, HEAD_DIM, HEAD_DIM), F32)],
        compiler_params=_cparams("parallel", "arbitrary"),
        name="gdn",
    )(proj, proj, proj, proj, small, conv_w, out_norm)


def _dsw_kernel(q_ref, kp_ref, kc_ref, vp_ref, vc_ref, o_ref, acc_ref, m_ref, l_ref):
    span = DSW_SPAN
    tile = DSW_TILE
    not_first = pl.program_id(2) > 0
    ri = lax.broadcasted_iota(jnp.int32, (span, 2 * span), 0)
    ci = lax.broadcasted_iota(jnp.int32, (span, 2 * span), 1)
    band = (ci >= ri) & (ci <= ri + span)
    band_first = band & ((ci >= span) | not_first)

    ones = jnp.ones((2 * span, HEAD_DIM), BF16)

    def unit(q, k, v, mask, rows, init):
        s = _dot_nt(q.astype(BF16), k.astype(BF16))
        s = jnp.where(mask, s, NEG_INF)
        mx = jnp.max(s, axis=-1, keepdims=True)
        va = jnp.concatenate([v.astype(BF16), ones], axis=-1)
        if init:
            m_ref[rows, :] = jnp.broadcast_to(mx, (span, LANES))
            pv = _dot(jnp.exp2(s - mx).astype(BF16), va)
            acc_ref[rows, :] = pv[:, :HEAD_DIM]
            l_ref[rows, :] = pv[:, HEAD_DIM:]
        else:
            m_old = m_ref[rows, :]
            m_new = jnp.maximum(m_old, mx)
            alpha = jnp.exp2(m_old - m_new)
            pv = _dot(jnp.exp2(s - _lane_tile(m_new, 2)).astype(BF16), va)
            m_ref[rows, :] = m_new
            acc_ref[rows, :] = alpha * acc_ref[rows, :] + pv[:, :HEAD_DIM]
            l_ref[rows, :] = alpha * l_ref[rows, :] + pv[:, HEAD_DIM:]

    for bi, (window, dil) in enumerate(DSW_PAIRS):
        assert window // dil == span
        n_sub = tile // (span * dil)
        prev0 = tile - span * dil
        for r in range(dil):
            def rows_of(start, n=span):
                return pl.ds(start, n, stride=dil) if dil > 1 else pl.ds(start, n)
            k = jnp.concatenate([kp_ref[rows_of(prev0 + r), :], kc_ref[rows_of(r), :]], axis=0)
            v = jnp.concatenate([vp_ref[rows_of(prev0 + r), :], vc_ref[rows_of(r), :]], axis=0)
            unit(q_ref[rows_of(r), :], k, v, band_first, rows_of(r), bi == 0)
            for i in range(1, n_sub):
                st = r + span * dil * i
                unit(q_ref[rows_of(st), :], kc_ref[rows_of(st - span * dil, 2 * span), :],
                     vc_ref[rows_of(st - span * dil, 2 * span), :], band, rows_of(st), bi == 0)
    o_ref[...] = acc_ref[...] / l_ref[...]


def _dsw(qk, proj, v_col):
    b, s, _ = qk.shape
    tile = DSW_TILE
    cur = lambda off: pl.BlockSpec((None, tile, HEAD_DIM), lambda bi, h, t: (bi, t, off + h))
    prev = lambda off: pl.BlockSpec((None, tile, HEAD_DIM), lambda bi, h, t: (bi, jnp.maximum(t - 1, 0), off + h))
    return pl.pallas_call(
        _dsw_kernel,
        grid=(b, DSW_HEADS, s // tile),
        in_specs=[cur(0), prev(DSW_HEADS), cur(DSW_HEADS), prev(v_col), cur(v_col)],
        out_specs=cur(0),
        out_shape=jax.ShapeDtypeStruct((b, s, DSW_HEADS * HEAD_DIM), F32),
        scratch_shapes=[pltpu.VMEM((tile, HEAD_DIM), F32)] * 3,
        compiler_params=_cparams("parallel", "parallel", "arbitrary"),
        name="dsw",
    )(qk, qk, qk, proj, proj)


def _nsa_compress_kernel(x_ref, pos_ref, w1_ref, w2_ref, o_ref, shift_ref):
    st = NSA_CMP_STRIDE
    n = x_ref.shape[0]
    p_acc = None
    q_acc = None
    for l in range(st):
        xl = x_ref[:, l * HEAD_DIM:(l + 1) * HEAD_DIM].astype(F32)
        pt = _dot((xl + pos_ref[l:l + 1, :]).astype(BF16), w1_ref[l])
        qt = _dot((xl + pos_ref[st + l:st + l + 1, :]).astype(BF16), w1_ref[st + l])
        p_acc = pt if p_acc is None else p_acc + pt
        q_acc = qt if q_acc is None else q_acc + qt
    shift_ref[0:n, :] = q_acc
    shift_ref[n:n + 8, :] = jnp.zeros((8, q_acc.shape[1]), F32)
    hid = p_acc + shift_ref[pl.ds(1, n), :]
    o_ref[...] = _dot(_silu(hid).astype(BF16), w2_ref[...]).astype(o_ref.dtype)


def _nsa_compress(src, pos, w1, w2):
    b, s, _ = src.shape
    st = NSA_CMP_STRIDE
    n = s // st
    hid = w1.shape[-1]
    x16 = src.reshape(b, n, st, NSA_KV_HEADS, HEAD_DIM).transpose(0, 3, 1, 2, 4).reshape(b, NSA_KV_HEADS, n, st * HEAD_DIM)
    return pl.pallas_call(
        _nsa_compress_kernel,
        grid=(b, NSA_KV_HEADS),
        in_specs=[pl.BlockSpec((None, None, n, st * HEAD_DIM), lambda bi, g: (bi, g, 0, 0)),
                  pl.BlockSpec((NSA_CMP_BLOCK, HEAD_DIM), lambda bi, g: (0, 0)),
                  pl.BlockSpec((NSA_CMP_BLOCK, HEAD_DIM, hid), lambda bi, g: (0, 0, 0)),
                  pl.BlockSpec((hid, HEAD_DIM), lambda bi, g: (0, 0))],
        out_specs=pl.BlockSpec((None, None, n, HEAD_DIM), lambda bi, g: (bi, g, 0, 0)),
        out_shape=jax.ShapeDtypeStruct((b, NSA_KV_HEADS, n, HEAD_DIM), BF16),
        scratch_shapes=[pltpu.VMEM((n + 8, hid), F32)],
        compiler_params=_cparams("parallel", "parallel"),
        name="nsa_compress",
    )(x16, pos, w1, w2)


NSA_ROWS = 64


def _stack_heads(q_ref, dst_ref, lanes=slice(0, HEAD_DIM)):
    tq = q_ref.shape[0]
    for h in range(NSA_HPG):
        dst_ref[h * tq:(h + 1) * tq, lanes] = q_ref[:, h * HEAD_DIM:(h + 1) * HEAD_DIM]


def _nsa_cmp_kernel(q_ref, kc_ref, vc_ref, ovt_ref, oc_ref, sel_ref, bias_ref, s_ref, p_ref, psum_ref):
    tq = q_ref.shape[0]
    ncp = kc_ref.shape[0]
    rb = NSA_ROWS
    t0 = pl.program_id(2) * tq
    tpos = t0 + lax.broadcasted_iota(jnp.int32, (tq, ncp), 0)
    cend = lax.broadcasted_iota(jnp.int32, (tq, ncp), 1) * NSA_CMP_STRIDE + (NSA_CMP_BLOCK - 1)
    bias_ref[...] = jnp.where(cend <= tpos, 0.0, NEG_INF)
    kc = kc_ref[...]
    vc = vc_ref[...]
    for h in range(NSA_HPG):
        sl = slice(h * HEAD_DIM, (h + 1) * HEAD_DIM)
        s_ref[h] = _dot_nt(q_ref[:, sl], kc)
        for rc in range(tq // rb):
            rows = pl.ds(rc * rb, rb)
            s = s_ref[h, rows, :] + bias_ref[rows, :]
            m = jnp.max(s, axis=-1, keepdims=True)
            e = jnp.exp2(s - m)
            inv = jnp.where(m > 0.5 * NEG_INF, 1.0 / jnp.sum(e, axis=-1, keepdims=True), 0.0)
            p = e * inv
            p_ref[h, rows, :] = p.astype(BF16)
            psum_ref[rows, :] = p if h == 0 else psum_ref[rows, :] + p
        oc_ref[:, sl] = _dot(p_ref[h], vc).astype(oc_ref.dtype)

    psum = psum_ref[...]
    p_hi = psum.astype(BF16)
    p_lo = (psum - p_hi.astype(F32)).astype(BF16)
    imp_t = _dot_nt(ovt_ref[...], p_hi) + _dot_nt(ovt_ref[...], p_lo)
    for cg in range(tq // LANES):
        jb = lax.broadcasted_iota(jnp.int32, (LANES, LANES), 0)
        tpos = t0 + cg * LANES + lax.broadcasted_iota(jnp.int32, (LANES, LANES), 1)
        cur = tpos // NSA_SLC_BLOCK
        forced = (jb == 0) | (jb == cur) | (jb == cur - 1)
        score = imp_t[:, cg * LANES:(cg + 1) * LANES] + jnp.where(forced, NSA_FORCE_BONUS, 0.0)
        score = jnp.where(jb <= cur, score, NEG_INF)
        taken = -3e38
        for _ in range(NSA_SLC_TOPK):
            top = jnp.max(score, axis=0, keepdims=True)
            idx = jnp.min(jnp.where(score == top, jb, LANES), axis=0, keepdims=True)
            score = jnp.where(jb == idx, taken, score)
        sel = jnp.where(score == taken, 0.0, NEG_INF)
        sel_ref[cg * LANES:(cg + 1) * LANES, :] = sel.T.astype(BF16)


NSA_KEY_TILE = 512


def _nsa_slc_kernel(q_ref, sel_ref, k_ref, v_ref, oh_ref, o_ref, qa_ref, s_ref, p_ref, acc_ref, m_ref):
    tq = q_ref.shape[0]
    tk = NSA_KEY_TILE
    rb = NSA_ROWS
    t0 = pl.program_id(2) * tq
    _stack_heads(q_ref, qa_ref)
    for h in range(NSA_HPG):
        qa_ref[h * tq:(h + 1) * tq, HEAD_DIM:] = sel_ref[...]
    acc_ref[...] = jnp.zeros_like(acc_ref)
    m_ref[...] = jnp.full(m_ref.shape, NEG_INF, F32)
    ones = jnp.ones((tk, HEAD_DIM), BF16)

    def tile(jt, on_diagonal):
        k0 = pl.multiple_of(jt * tk, tk)
        ka = jnp.concatenate([k_ref[pl.ds(k0, tk), :], oh_ref[pl.ds(k0, tk), :]], axis=-1)
        va = jnp.concatenate([v_ref[pl.ds(k0, tk), :], ones], axis=-1)
        for h in range(NSA_HPG):
            s_ref[h] = _dot_nt(qa_ref[h * tq:(h + 1) * tq, :], ka)
            for rc in range(tq // rb):
                rows = pl.ds(rc * rb, rb)
                s = s_ref[h, rows, :]
                if on_diagonal:
                    tpos = t0 + rc * rb + lax.broadcasted_iota(jnp.int32, (rb, tk), 0)
                    s = jnp.where(k0 + lax.broadcasted_iota(jnp.int32, (rb, tk), 1) <= tpos, s, NEG_INF)
                m_old = m_ref[h, rows, :]
                m_new = jnp.maximum(m_old, jnp.max(s, axis=-1, keepdims=True))
                p = jnp.exp2(s - _lane_tile(m_new, tk // LANES))
                m_ref[h, rows, :] = m_new
                acc_ref[h, rows, :] = _lane_tile(jnp.exp2(m_old - m_new), 2) * acc_ref[h, rows, :]
                p_ref[h, rows, :] = p.astype(BF16)
            acc_ref[h] += _dot(p_ref[h], va)

    n_below = t0 // tk

    def pair(jp, carry):
        tile(2 * jp, False)
        tile(2 * jp + 1, False)
        return carry

    lax.fori_loop(0, n_below // 2, pair, 0)

    @pl.when(n_below % 2 == 1)
    def _():
        tile(n_below - 1, False)

    tile(n_below, True)
    for h in range(NSA_HPG):
        o_ref[:, h * HEAD_DIM:(h + 1) * HEAD_DIM] = (acc_ref[h, :, :HEAD_DIM] / acc_ref[h, :, HEAD_DIM:]).astype(o_ref.dtype)


def _nsa_win_kernel(q_ref, k_ref, v_ref, o_ref, bias_ref, s_ref, p_ref):
    tq = q_ref.shape[0]
    nk = NSA_WINDOW + tq
    rb = NSA_ROWS
    t0 = pl.program_id(2) * tq
    start = pl.multiple_of(jnp.maximum(t0 - NSA_WINDOW, 0), tq)
    tpos = t0 + lax.broadcasted_iota(jnp.int32, (tq, nk), 0)
    kpos = start + lax.broadcasted_iota(jnp.int32, (tq, nk), 1)
    bias_ref[...] = jnp.where((kpos <= tpos) & (kpos > tpos - NSA_WINDOW), 0.0, NEG_INF)
    kt = k_ref[pl.ds(start, nk), :]
    va = jnp.concatenate([v_ref[pl.ds(start, nk), :], jnp.ones((nk, HEAD_DIM), BF16)], axis=-1)
    for h in range(NSA_HPG):
        sl = slice(h * HEAD_DIM, (h + 1) * HEAD_DIM)
        s_ref[h] = _dot_nt(q_ref[:, sl], kt)
        for rc in range(tq // rb):
            rows = pl.ds(rc * rb, rb)
            s = s_ref[h, rows, :] + bias_ref[rows, :]
            p_ref[h, rows, :] = jnp.exp2(s - jnp.max(s, axis=-1, keepdims=True)).astype(BF16)
        pv = _dot(p_ref[h], va)
        o_ref[:, sl] = (pv[:, :HEAD_DIM] / pv[:, HEAD_DIM:]).astype(o_ref.dtype)


def _nsa_overlap_t(n_cmp_pad, n_cmp):
    c_start = np.arange(n_cmp_pad)[None, :] * NSA_CMP_STRIDE
    s_start = np.arange(LANES)[:, None] * NSA_SLC_BLOCK
    ov = (c_start < s_start + NSA_SLC_BLOCK) & (c_start + NSA_CMP_BLOCK > s_start)
    ov &= (np.arange(n_cmp_pad) < n_cmp)[None, :]
    return jnp.asarray(ov, BF16)


def _nsa_attention(hd, pv, kc, vc, tq_cmp=512, tq_slc=512, tq_win=256):
    b, s, _ = hd.shape
    g = NSA_KV_HEADS
    qw = NSA_HPG * HEAD_DIM
    ncp = s // NSA_CMP_STRIDE
    nk_win = NSA_WINDOW + tq_win
    assert s // NSA_SLC_BLOCK <= LANES and s >= nk_win and NSA_KEY_TILE % tq_slc == 0
    grid = lambda tq: (b, g, s // tq)
    q_spec = lambda tq: pl.BlockSpec((None, tq, qw), lambda bi, gi, i: (bi, i, gi))
    seq_spec = lambda off: pl.BlockSpec((None, s, HEAD_DIM), lambda bi, gi, i: (bi, 0, off + gi))
    cmp_spec = pl.BlockSpec((None, None, ncp, HEAD_DIM), lambda bi, gi, i: (bi, gi, 0, 0))
    sel_spec = lambda tq: pl.BlockSpec((None, None, tq, LANES), lambda bi, gi, i: (bi, gi, i, 0))
    const_spec = lambda shape: pl.BlockSpec(shape, lambda bi, gi, i: (0, 0))
    out_sd = jax.ShapeDtypeStruct((b, s, NSA_HEADS * HEAD_DIM), BF16)
    sel_sd = jax.ShapeDtypeStruct((b, g, s, LANES), BF16)
    sem = ("parallel", "parallel", "arbitrary")
    k_slc, k_win = NSA_HEADS + g, NSA_HEADS + 2 * g
    v_slc, v_win = g, 2 * g
    block_onehot = jnp.asarray(np.arange(s)[:, None] // NSA_SLC_BLOCK == np.arange(LANES)[None, :], BF16)

    rows = NSA_HPG * tq_cmp
    o_c, sel = pl.pallas_call(
        _nsa_cmp_kernel, grid=grid(tq_cmp),
        in_specs=[q_spec(tq_cmp), cmp_spec, cmp_spec, const_spec((LANES, ncp))],
        out_specs=[q_spec(tq_cmp), sel_spec(tq_cmp)], out_shape=[out_sd, sel_sd],
        scratch_shapes=[pltpu.VMEM((tq_cmp, ncp), F32), pltpu.VMEM((NSA_HPG, tq_cmp, ncp), F32),
                        pltpu.VMEM((NSA_HPG, tq_cmp, ncp), BF16), pltpu.VMEM((tq_cmp, ncp), F32)],
        compiler_params=_cparams(*sem), name="nsa_cmp",
    )(hd, kc, vc, _nsa_overlap_t(ncp, ncp - 1))
    tq = tq_slc
    o_s = pl.pallas_call(
        _nsa_slc_kernel, grid=grid(tq),
        in_specs=[q_spec(tq), sel_spec(tq), seq_spec(k_slc), seq_spec(v_slc), const_spec((s, LANES))],
        out_specs=q_spec(tq), out_shape=out_sd,
        scratch_shapes=[pltpu.VMEM((NSA_HPG * tq, 2 * HEAD_DIM), BF16), pltpu.VMEM((NSA_HPG, tq, NSA_KEY_TILE), F32),
                        pltpu.VMEM((NSA_HPG, tq, NSA_KEY_TILE), BF16), pltpu.VMEM((NSA_HPG, tq, 2 * HEAD_DIM), F32),
                        pltpu.VMEM((NSA_HPG, tq, HEAD_DIM), F32)],
        compiler_params=_cparams(*sem), name="nsa_slc",
    )(hd, sel, hd, pv, block_onehot)
    rows = NSA_HPG * tq_win
    o_w = pl.pallas_call(
        _nsa_win_kernel, grid=grid(tq_win),
        in_specs=[q_spec(tq_win), seq_spec(k_win), seq_spec(v_win)],
        out_specs=q_spec(tq_win), out_shape=out_sd,
        scratch_shapes=[pltpu.VMEM((tq_win, nk_win), F32), pltpu.VMEM((NSA_HPG, tq_win, nk_win), F32),
                        pltpu.VMEM((NSA_HPG, tq_win, nk_win), BF16)],
        compiler_params=_cparams(*sem), name="nsa_win",
    )(hd, hd, pv)
    return o_c, o_s, o_w


def _row(v, width=None):
    v = v.reshape(1, -1).astype(F32)
    if width is not None and v.shape[1] < width:
        v = jnp.pad(v, ((0, 0), (0, width - v.shape[1])))
    return v


def _even_mixer(xf, b, s, gain, w_in, w_out, conv_w, a_log, dt_bias, gdn_norm, q_norm, k_norm, rope):
    gw = GDN_HEADS * HEAD_DIM
    dw = DSW_HEADS * HEAD_DIM
    o_small = 4 * gw
    o_dsw = o_small + 2 * GDN_HEADS
    w_plain = jnp.concatenate([w_in[:, :o_small], w_in[:, o_dsw + 2 * dw:]], axis=1).astype(BF16)
    w_heads = w_in[:, o_dsw:o_dsw + 2 * dw].astype(BF16)
    w_small = jnp.pad(w_in[:, o_small:o_dsw], ((0, 0), (0, LANES - 2 * GDN_HEADS))).astype(BF16)
    pad8 = lambda v: jnp.pad(_row(v), ((0, 0), (GDN_HEADS, LANES - 2 * GDN_HEADS)))
    head_gain = jnp.concatenate([jnp.tile(_row(q_norm) * Q_SCALE_LOG2, (1, DSW_HEADS)),
                                 jnp.tile(_row(k_norm), (1, DSW_HEADS))], axis=1)

    proj = _norm_proj(xf, gain, w_plain, "plain", tm=1024, tn=1024)
    small = _norm_proj(xf, gain, w_small, "gdn_gates", (pad8(a_log), pad8(dt_bias)))
    qk = _norm_proj_heads(xf, gain, w_heads, head_gain, rope, F32)
    cw = conv_w.reshape(GDN_CONV, 3, gw).transpose(1, 0, 2)
    o_a = _gdn(proj.reshape(b, s, -1), small.reshape(b, s, LANES), cw, _row(gdn_norm))
    o_b = _dsw(qk.reshape(b, s, -1), proj.reshape(b, s, -1), o_small // HEAD_DIM)
    t = b * s
    return _proj_res(xf, [o_a.reshape(t, gw), o_b.reshape(t, dw)],
                     [w_out[:gw].astype(BF16), w_out[gw:].astype(BF16)])


def _odd_mixer(xf, b, s, gain, w_in, w_out, q_norm, k_norm, cmp_pos, cmp_w1, cmp_w2, rope):
    qw = NSA_HEADS * HEAD_DIM
    kvw = NSA_KV_HEADS * HEAD_DIM
    col = lambda i: slice(qw + i * kvw, qw + (i + 1) * kvw)
    w_heads = jnp.concatenate([w_in[:, :qw], w_in[:, col(0)], w_in[:, col(2)], w_in[:, col(4)]], axis=1).astype(BF16)
    w_plain = jnp.concatenate([w_in[:, col(1)], w_in[:, col(3)], w_in[:, col(5)]], axis=1).astype(BF16)
    w_gate = jnp.pad(w_in[:, qw + 6 * kvw:], ((0, 0), (0, LANES - 3 * NSA_HEADS))).astype(BF16)
    head_gain = jnp.concatenate([jnp.tile(_row(q_norm) * Q_SCALE_LOG2, (1, NSA_HEADS))]
                                + [jnp.tile(_row(k_norm[i]), (1, NSA_KV_HEADS)) for i in range(3)], axis=1)

    hd = _norm_proj_heads(xf, gain, w_heads, head_gain, rope, BF16).reshape(b, s, -1)
    pv = _norm_proj(xf, gain, w_plain, "plain", out_dtype=BF16, tm=1024).reshape(b, s, -1)
    gates = _norm_proj(xf, gain, w_gate, "sigmoid")
    hid = cmp_w1.shape[-1]
    w1 = cmp_w1.reshape(2, NSA_CMP_BLOCK, HEAD_DIM, hid).astype(BF16)
    kc = _nsa_compress(hd[:, :, qw:qw + kvw], _perm_head_dims(cmp_pos[0]), _perm_head_dims(w1[0], axis=1),
                       _perm_head_dims(cmp_w2[0].astype(BF16)))
    vc = _nsa_compress(pv[:, :, :kvw], cmp_pos[1], w1[1], cmp_w2[1].astype(BF16))
    o_c, o_s, o_w = _nsa_attention(hd, pv, kc, vc)
    t = b * s
    return _proj_res(xf, [o_c.reshape(t, qw), o_s.reshape(t, qw), o_w.reshape(t, qw)], [w_out.astype(BF16)], gates)


def kernel(x, mem, positions, ffn1_norm, ffn1_w_gu, ffn1_w_down, mix_norm, ev_w_in, ev_w_out, gdn_conv_w, gdn_a_log, gdn_dt_bias, gdn_out_norm, dsw_q_norm, dsw_k_norm, od_w_in, od_w_out, nsa_q_norm, nsa_k_norm, nsa_cmp_pos, nsa_cmp_w1, nsa_cmp_w2, xa_norm, xa_mem_norm, xa_w_q, xa_w_kv, xa_q_norm, xa_k_norm, xa_w_o, ffn2_norm, ffn2_w_gu, ffn2_w_down):
    b, s, d = x.shape
    depth = ffn1_norm.shape[0]
    rope = _rope_tables(positions)
    xf = x.reshape(b * s, d)
    for i in range(depth):
        xf = _ffn(xf, _row(ffn1_norm[i]), ffn1_w_gu[i].astype(BF16), ffn1_w_down[i].astype(BF16))
        if i % 2 == 0:
            e = i // 2
            xf = _even_mixer(xf, b, s, _row(mix_norm[i]), ev_w_in[e], ev_w_out[e], gdn_conv_w[e], gdn_a_log[e],
                             gdn_dt_bias[e], gdn_out_norm[e], dsw_q_norm[e], dsw_k_norm[e], rope)
        else:
            o = i // 2
            xf = _odd_mixer(xf, b, s, _row(mix_norm[i]), od_w_in[o], od_w_out[o], nsa_q_norm[o], nsa_k_norm[o],
                            nsa_cmp_pos[o], nsa_cmp_w1[o], nsa_cmp_w2[o], rope)
        xf = _cross_attention(xf.reshape(b, s, d), mem, _row(xa_norm[i]), _row(xa_mem_norm[i]),
                              xa_w_q[i].astype(BF16), xa_w_kv[i].astype(BF16), _row(xa_q_norm[i]),
                              _row(xa_k_norm[i]), xa_w_o[i].astype(BF16)).reshape(b * s, d)
        xf = _ffn(xf, _row(ffn2_norm[i]), ffn2_w_gu[i].astype(BF16), ffn2_w_down[i].astype(BF16))
    return xf.reshape(b, s, d)
```

```python
import functools

import numpy as np
import jax
import jax.numpy as jnp
from jax import lax
from jax.experimental import pallas as pl
from jax.experimental.pallas import tpu as pltpu

F32 = jnp.float32
BF16 = jnp.bfloat16
HI = lax.Precision.HIGHEST

EPS = 1e-6
NEG_INF = -1e30
HEAD_DIM = 128
ROPE_THETA = 500000.0
ROPE_DIM = HEAD_DIM // 4
ATT_SCALE = HEAD_DIM ** -0.5
LOG2E = float(np.log2(np.e))
Q_SCALE_LOG2 = ATT_SCALE * LOG2E

GDN_HEADS = 8
GDN_CHUNK = 64
GDN_CONV = 4
DSW_HEADS = 8
DSW_PAIRS = ((128, 1), (512, 4), (2048, 16))
DSW_SPAN = 128
DSW_TILE = 2048
NSA_HEADS = 16
NSA_KV_HEADS = 4
NSA_HPG = NSA_HEADS // NSA_KV_HEADS
NSA_CMP_BLOCK = 32
NSA_CMP_STRIDE = 16
NSA_SLC_BLOCK = 64
NSA_SLC_TOPK = 16
NSA_WINDOW = 512
NSA_FORCE_BONUS = 1e3
XA_HEADS = 4

VMEM_LIMIT_BYTES = 56 * 1024 * 1024
LANES = 128


def _cparams(*sem):
    return pltpu.CompilerParams(dimension_semantics=sem, vmem_limit_bytes=VMEM_LIMIT_BYTES)


def _dot(a, b, precision=None):
    return jnp.dot(a, b, preferred_element_type=F32, precision=precision)


def _dot_nt(a, b, precision=None):
    return lax.dot_general(a, b, (((1,), (1,)), ((), ())), preferred_element_type=F32, precision=precision)


def _dot_tn(a, b, precision=None):
    return lax.dot_general(a, b, (((0,), (0,)), ((), ())), preferred_element_type=F32, precision=precision)


def _bmm(a, b, dn):
    return lax.dot_general(a.astype(BF16), b.astype(BF16), dn, preferred_element_type=F32)


def _bdot(a, b):
    return _bmm(a, b, (((2,), (1,)), ((0,), (0,))))


def _bdot_nt(a, b):
    return _bmm(a, b, (((2,), (2,)), ((0,), (0,))))


def _bdot_tn(a, b):
    return _bmm(a, b, (((1,), (1,)), ((0,), (0,))))


def _rms(x, gain):
    return x * lax.rsqrt(jnp.mean(x * x, axis=-1, keepdims=True) + EPS) * gain


def _silu(x):
    return x * jax.nn.sigmoid(x)


def _lane_tile(x, n):
    return jnp.concatenate([x] * n, axis=1)


def _ffn_kernel(x_ref, g_ref, wg_ref, wu_ref, wd_ref, o_ref, xn_ref, acc_ref):
    f = pl.program_id(1)

    @pl.when(f == 0)
    def _():
        xn_ref[...] = _rms(x_ref[...], g_ref[...]).astype(BF16)
        acc_ref[...] = jnp.zeros_like(acc_ref)

    xn = xn_ref[...]
    gate = _dot(xn, wg_ref[...])
    up = _dot(xn, wu_ref[...])
    act = (_silu(gate) * up).astype(BF16)
    acc_ref[...] += _dot(act, wd_ref[...])

    @pl.when(f == pl.num_programs(1) - 1)
    def _():
        o_ref[...] = x_ref[...] + 0.5 * acc_ref[...]


def _ffn(x, gain, w_gu, w_down, tm=512, tf=512):
    t, d = x.shape
    d_ff = w_down.shape[0]
    n_f = d_ff // tf
    return pl.pallas_call(
        _ffn_kernel,
        grid=(t // tm, n_f),
        in_specs=[
            pl.BlockSpec((tm, d), lambda i, f: (i, 0)),
            pl.BlockSpec((1, d), lambda i, f: (0, 0)),
            pl.BlockSpec((d, tf), lambda i, f: (0, f)),
            pl.BlockSpec((d, tf), lambda i, f: (0, f + n_f)),
            pl.BlockSpec((tf, d), lambda i, f: (f, 0)),
        ],
        out_specs=pl.BlockSpec((tm, d), lambda i, f: (i, 0)),
        out_shape=jax.ShapeDtypeStruct((t, d), F32),
        scratch_shapes=[pltpu.VMEM((tm, d), BF16), pltpu.VMEM((tm, d), F32)],
        compiler_params=_cparams("parallel", "arbitrary"),
        name="ffn",
    )(x, gain, w_gu, w_gu, w_down)


ROPE_HALF = ROPE_DIM // 2
ROPE_SHIFT = LANES // 2
ROPE_PERM = np.arange(HEAD_DIM)
ROPE_PERM[ROPE_HALF:ROPE_DIM] = np.arange(ROPE_SHIFT, ROPE_SHIFT + ROPE_HALF)
ROPE_PERM[ROPE_SHIFT:ROPE_SHIFT + ROPE_HALF] = np.arange(ROPE_HALF, ROPE_DIM)


def _perm_head_dims(a, axis=-1):
    n_heads = a.shape[axis] // HEAD_DIM
    idx = (np.arange(n_heads)[:, None] * HEAD_DIM + ROPE_PERM[None, :]).reshape(-1)
    return jnp.take(a, idx, axis=axis)


def _rope_kernel(pos_ref, invf_ref, c_ref, s_ref):
    ang = pos_ref[...] * invf_ref[...]
    lane = lax.broadcasted_iota(jnp.int32, ang.shape, 1)
    sin = jnp.sin(ang)
    c_ref[...] = jnp.cos(ang)
    s_ref[...] = jnp.where(lane < ROPE_SHIFT, -sin, sin)


def _rope_tables(positions, tm=1024):
    t = positions.size
    inv_freq = jnp.float32(ROPE_THETA) ** (-jnp.arange(ROPE_HALF, dtype=F32) / ROPE_HALF)
    invf = jnp.zeros((1, LANES), F32).at[0, :ROPE_HALF].set(inv_freq).at[0, ROPE_SHIFT:ROPE_SHIFT + ROPE_HALF].set(inv_freq)
    pos = positions.astype(F32).reshape(t, 1)
    spec = pl.BlockSpec((tm, LANES), lambda i: (i, 0))
    return pl.pallas_call(
        _rope_kernel,
        grid=(t // tm,),
        in_specs=[pl.BlockSpec((tm, 1), lambda i: (i, 0)), pl.BlockSpec((1, LANES), lambda i: (0, 0))],
        out_specs=[spec, spec],
        out_shape=[jax.ShapeDtypeStruct((t, LANES), F32)] * 2,
        compiler_params=_cparams("parallel"),
        name="rope_tables",
    )(pos, invf)


def _rope(x, c, s):
    return x * c + pltpu.roll(x, ROPE_SHIFT, 1) * s


def _norm_proj_kernel(x_ref, g_ref, w_ref, o_ref, xn_ref):
    @pl.when(pl.program_id(1) == 0)
    def _():
        xn_ref[...] = _rms(x_ref[...], g_ref[...]).astype(BF16)

    o_ref[...] = _dot(xn_ref[...], w_ref[...]).astype(o_ref.dtype)


def _norm_proj(x, gain, w, out_dtype=F32, tm=1024, tn=512):
    t, d = x.shape
    n = w.shape[1]
    tn = min(tn, n)
    assert t % tm == 0 and n % tn == 0
    return pl.pallas_call(
        _norm_proj_kernel,
        grid=(t // tm, n // tn),
        in_specs=[pl.BlockSpec((tm, d), lambda i, j: (i, 0)),
                  pl.BlockSpec((1, d), lambda i, j: (0, 0)),
                  pl.BlockSpec((d, tn), lambda i, j: (0, j))],
        out_specs=pl.BlockSpec((tm, tn), lambda i, j: (i, j)),
        out_shape=jax.ShapeDtypeStruct((t, n), out_dtype),
        scratch_shapes=[pltpu.VMEM((tm, d), BF16)],
        compiler_params=_cparams("parallel", "arbitrary"),
        name="norm_proj_plain",
    )(x, gain, w)


def _norm_proj_heads_kernel(gate_mode, x_ref, g_ref, w_ref, hg_ref, c_ref, s_ref, wg_ref, *rest):
    o_ref, og_ref = rest[-2], rest[-1]
    xn = _rms(x_ref[...], g_ref[...]).astype(BF16)
    c, s = c_ref[...], s_ref[...]
    pair = 2 * HEAD_DIM
    for kp in range(w_ref.shape[1] // pair):
        y = _dot(xn, w_ref[:, kp * pair:(kp + 1) * pair])
        for k in range(2):
            sl = slice(kp * pair + k * HEAD_DIM, kp * pair + (k + 1) * HEAD_DIM)
            yk = y[:, k * HEAD_DIM:(k + 1) * HEAD_DIM]
            o_ref[:, sl] = _rope(_rms(yk, hg_ref[:, sl]), c, s).astype(o_ref.dtype)
    y = _dot(xn, wg_ref[...])
    if gate_mode == "sigmoid":
        og_ref[...] = jax.nn.sigmoid(y)
    else:
        alog_ref, dt_ref = rest[0], rest[1]
        lane = lax.broadcasted_iota(jnp.int32, y.shape, 1)
        z = y + dt_ref[...]
        softplus = jnp.maximum(z, 0.0) + jnp.log1p(jnp.exp(-jnp.abs(z)))
        og_ref[...] = jnp.where(lane < GDN_HEADS, jax.nn.sigmoid(y), -jnp.exp(alog_ref[...]) * softplus)


def _norm_proj_heads(x, gain, w, head_gain, rope, out_dtype, w_gate, gate_mode, gate_rows=(), tm=512):
    t, d = x.shape
    n = w.shape[1]
    assert t % tm == 0 and n % (2 * HEAD_DIM) == 0 and w_gate.shape == (d, LANES)
    assert gate_mode in ("sigmoid", "gdn_gates")
    w = _perm_head_dims(w)
    head_gain = _perm_head_dims(head_gain)
    full = lambda shape: pl.BlockSpec(shape, lambda i: (0, 0))
    row = lambda width: pl.BlockSpec((tm, width), lambda i: (i, 0))
    return pl.pallas_call(
        functools.partial(_norm_proj_heads_kernel, gate_mode),
        grid=(t // tm,),
        in_specs=[row(d), full((1, d)), full((d, n)), full((1, n)), row(LANES), row(LANES), full((d, LANES))]
                 + [full((1, LANES))] * len(gate_rows),
        out_specs=[row(n), row(LANES)],
        out_shape=[jax.ShapeDtypeStruct((t, n), out_dtype), jax.ShapeDtypeStruct((t, LANES), F32)],
        compiler_params=_cparams("parallel"),
        name="norm_proj_heads",
    )(x, gain, w, head_gain, *rope, w_gate, *gate_rows)


def _proj_res_kernel(n_in, gated, *refs):
    x_ref = refs[0]
    a_refs = refs[1:1 + n_in]
    o_ref = refs[-1]
    w_refs = refs[1 + n_in:-2] if gated else refs[1 + n_in:-1]
    if gated:
        gt = refs[-2][...]
        cols = []
        for h in range(NSA_HEADS):
            sl = slice(h * HEAD_DIM, (h + 1) * HEAD_DIM)
            acc = None
            for j in range(n_in):
                term = gt[:, 3 * h + j:3 * h + j + 1] * a_refs[j][:, sl]
                acc = term if acc is None else acc + term
            cols.append(acc.astype(BF16))
        y = _dot(jnp.concatenate(cols, axis=-1), w_refs[0][...])
    else:
        y = None
        for a_ref, w_ref in zip(a_refs, w_refs):
            term = _dot(a_ref[...].astype(BF16), w_ref[...])
            y = term if y is None else y + term
    o_ref[...] = x_ref[...] + y


def _proj_res(x, acts, ws, gates=None, tm=512):
    t, d = x.shape
    n_in = len(acts)
    row = lambda i: (i, 0)
    in_specs = [pl.BlockSpec((tm, d), row)]
    in_specs += [pl.BlockSpec((tm, a.shape[1]), row) for a in acts]
    in_specs += [pl.BlockSpec(w.shape, lambda i: (0, 0)) for w in ws]
    args = [x, *acts, *ws]
    if gates is not None:
        in_specs.append(pl.BlockSpec((tm, LANES), row))
        args.append(gates)
    return pl.pallas_call(
        functools.partial(_proj_res_kernel, n_in, gates is not None),
        grid=(t // tm,),
        in_specs=in_specs,
        out_specs=pl.BlockSpec((tm, d), row),
        out_shape=jax.ShapeDtypeStruct((t, d), F32),
        compiler_params=_cparams("parallel"),
        name="proj_res",
    )(*args)


def _xa_kv_kernel(mem_ref, gm_ref, w_ref, kn_ref, k_ref, v_ref):
    kv = _dot(_rms(mem_ref[...], gm_ref[...]).astype(BF16), w_ref[...])
    xa_w = XA_HEADS * HEAD_DIM
    for h in range(XA_HEADS):
        sl = slice(h * HEAD_DIM, (h + 1) * HEAD_DIM)
        k_ref[:, sl] = _rms(kv[:, sl], kn_ref[...]).astype(BF16)
    v_ref[...] = kv[:, xa_w:].astype(BF16)


def _xa_kernel(x_ref, g_ref, wq_ref, qn_ref, k_ref, v_ref, wo_ref, o_ref):
    x = x_ref[...]
    q = _dot(_rms(x, g_ref[...]).astype(BF16), wq_ref[...])
    outs = []
    for h in range(XA_HEADS):
        sl = slice(h * HEAD_DIM, (h + 1) * HEAD_DIM)
        qh = (_rms(q[:, sl], qn_ref[...]) * Q_SCALE_LOG2).astype(BF16)
        s = _dot_nt(qh, k_ref[:, sl])
        p = jnp.exp2(s - jnp.max(s, axis=-1, keepdims=True))
        l = jnp.sum(p, axis=-1, keepdims=True)
        outs.append((_dot(p.astype(BF16), v_ref[:, sl]) / l).astype(BF16))
    o_ref[...] = x + _dot(jnp.concatenate(outs, axis=-1), wo_ref[...])


def _cross_attention(x, mem, g_x, g_mem, w_q, w_kv, q_norm, k_norm, w_o, tm=512):
    b, s, d = x.shape
    m_len = mem.shape[1]
    xa_w = XA_HEADS * HEAD_DIM
    full = lambda shape: pl.BlockSpec(shape, lambda *_: (0,) * len(shape))
    k, v = pl.pallas_call(
        _xa_kv_kernel,
        grid=(b,),
        in_specs=[pl.BlockSpec((None, m_len, d), lambda i: (i, 0, 0)), full((1, d)), full((d, 2 * xa_w)),
                  full((1, HEAD_DIM))],
        out_specs=[pl.BlockSpec((None, m_len, xa_w), lambda i: (i, 0, 0))] * 2,
        out_shape=[jax.ShapeDtypeStruct((b, m_len, xa_w), BF16)] * 2,
        compiler_params=_cparams("parallel"),
        name="xa_kv",
    )(mem, g_mem, w_kv, k_norm)
    return pl.pallas_call(
        _xa_kernel,
        grid=(b, s // tm),
        in_specs=[pl.BlockSpec((None, tm, d), lambda i, j: (i, j, 0)), full((1, d)), full((d, xa_w)),
                  full((1, HEAD_DIM)),
                  pl.BlockSpec((None, m_len, xa_w), lambda i, j: (i, 0, 0)),
                  pl.BlockSpec((None, m_len, xa_w), lambda i, j: (i, 0, 0)),
                  full((xa_w, d))],
        out_specs=pl.BlockSpec((None, tm, d), lambda i, j: (i, j, 0)),
        out_shape=jax.ShapeDtypeStruct((b, s, d), F32),
        compiler_params=_cparams("parallel", "parallel"),
        name="cross_attn",
    )(x, g_x, w_q, q_norm, k, v, w_o)


def _gdn_kernel(q_ref, k_ref, v_ref, z_ref, sm_ref, cw_ref, gn_ref, o_ref, halo_ref, xbuf_ref, qkv_ref, state_ref):
    c = GDN_CHUNK
    rows = q_ref.shape[0]
    n_ck = rows // c

    @pl.when(pl.program_id(1) == 0)
    def _():
        halo_ref[...] = jnp.zeros_like(halo_ref)
        state_ref[...] = jnp.zeros_like(state_ref)

    for idx, src in enumerate((q_ref, k_ref, v_ref)):
        xbuf_ref[0:8, :] = halo_ref[idx]
        xbuf_ref[8:8 + rows, :] = src[...]
        halo_ref[idx] = src[rows - 8:rows, :]
        w = cw_ref[idx]
        acc = xbuf_ref[8:8 + rows, :] * w[GDN_CONV - 1:GDN_CONV, :]
        for tap in range(GDN_CONV - 1):
            off = 8 - (GDN_CONV - 1) + tap
            acc = acc + xbuf_ref[off:off + rows, :] * w[tap:tap + 1, :]
        qkv_ref[idx] = _silu(acc)

    row = lax.broadcasted_iota(jnp.int32, (c, c), 0)
    col = lax.broadcasted_iota(jnp.int32, (c, c), 1)
    causal = row >= col
    strict = row > col
    eye_f = (row == col).astype(F32)
    tril_f = causal.astype(F32)
    triu_f = (col >= row).astype(F32)
    nh = GDN_HEADS
    csl = lambda ck: slice(ck * c, (ck + 1) * c)
    hsl = lambda h: slice(h * HEAD_DIM, (h + 1) * HEAD_DIM)
    sm = sm_ref[...]
    gam_cols = [_dot(tril_f, sm[csl(ck)], HI) for ck in range(n_ck)]
    gam_rows = [_dot_tn(sm[csl(ck)], triu_f, HI) for ck in range(n_ck)]

    pairs = [(ck, h) for ck in range(n_ck) for h in range(nh)]
    q = jnp.stack([qkv_ref[0, csl(ck), hsl(h)] for ck, h in pairs])
    k = jnp.stack([qkv_ref[1, csl(ck), hsl(h)] for ck, h in pairs])
    v = jnp.stack([qkv_ref[2, csl(ck), hsl(h)] for ck, h in pairs])
    beta = jnp.stack([sm[csl(ck), h:h + 1] for ck, h in pairs])
    gam = jnp.stack([gam_cols[ck][:, nh + h:nh + h + 1] for ck, h in pairs])
    gam_r = jnp.stack([gam_rows[ck][nh + h:nh + h + 1, :] for ck, h in pairs])
    q = q * lax.rsqrt(jnp.sum(q * q, axis=-1, keepdims=True) + EPS) * ATT_SCALE
    k = k * lax.rsqrt(jnp.sum(k * k, axis=-1, keepdims=True) + EPS)
    gam_last = gam[:, c - 1:c, :]
    eg = jnp.exp(gam)
    decay = jnp.where(causal, jnp.exp(jnp.where(causal, gam - gam_r, 0.0)), 0.0)
    kb = k * beta
    a_mat = jnp.where(strict, _bdot_nt(kb, k) * decay, 0.0)
    pw = -a_mat
    inv = eye_f + pw
    for _ in range(5):
        pw = _bdot(pw, pw)
        inv = inv + _bdot(inv, pw)
    uw = _bdot(inv, jnp.concatenate([v * beta, kb * eg], axis=-1))
    u, w = uw[..., :HEAD_DIM], uw[..., HEAD_DIM:]
    qk = jnp.where(causal, _bdot_nt(q, k) * decay, 0.0)
    q_dec = q * eg
    k_dec = k * jnp.exp(gam_last - gam)
    chunk_decay = jnp.exp(gam_last)
    state = state_ref[...]
    for ck in range(n_ck):
        bs = slice(ck * nh, (ck + 1) * nh)
        v_new = u[bs] - _bdot(w[bs], state)
        out = _rms(_bdot(q_dec[bs], state) + _bdot(qk[bs], v_new), gn_ref[...])
        state = state * chunk_decay[bs] + _bdot_tn(k_dec[bs], v_new)
        for h in range(nh):
            o_ref[csl(ck), hsl(h)] = out[h] * _silu(z_ref[csl(ck), hsl(h)])
    state_ref[...] = state


GDN_STEP_CHUNKS = 4


def _gdn(proj, small, conv_w, out_norm):
    b, s, _ = proj.shape
    c = GDN_CHUNK * GDN_STEP_CHUNKS
    assert s % c == 0
    w_h = GDN_HEADS * HEAD_DIM
    blk = lambda j: pl.BlockSpec((None, c, w_h), lambda bi, i: (bi, i, j))
    return pl.pallas_call(
        _gdn_kernel,
        grid=(b, s // c),
        in_specs=[blk(0), blk(1), blk(2), blk(3),
                  pl.BlockSpec((None, c, LANES), lambda bi, i: (bi, i, 0)),
                  pl.BlockSpec((3, GDN_CONV, w_h), lambda bi, i: (0, 0, 0)),
                  pl.BlockSpec((1, HEAD_DIM), lambda bi, i: (0, 0))],
        out_specs=pl.BlockSpec((None, c, w_h), lambda bi, i: (bi, i, 0)),
        out_shape=jax.ShapeDtypeStruct((b, s, w_h), F32),
        scratch_shapes=[pltpu.VMEM((3, 8, w_h), F32), pltpu.VMEM((c + 8, w_h), F32),
                        pltpu.VMEM((3, c, w_h), F32), pltpu.VMEM((GDN_HEADS, HEAD_DIM, HEAD_DIM), F32)],
        compiler_params=_cparams("parallel", "arbitrary"),
        name="gdn",
    )(proj, proj, proj, proj, small, conv_w, out_norm)


def _dsw_kernel(q_ref, kp_ref, kc_ref, vp_ref, vc_ref, o_ref, acc_ref, m_ref, l_ref):
    span = DSW_SPAN
    tile = DSW_TILE
    not_first = pl.program_id(2) > 0
    ri = lax.broadcasted_iota(jnp.int32, (span, 2 * span), 0)
    ci = lax.broadcasted_iota(jnp.int32, (span, 2 * span), 1)
    band = (ci >= ri) & (ci <= ri + span)
    band_first = band & ((ci >= span) | not_first)

    ones = jnp.ones((2 * span, HEAD_DIM), BF16)

    def unit(q, k, v, mask, rows, init):
        s = _dot_nt(q.astype(BF16), k.astype(BF16))
        s = jnp.where(mask, s, NEG_INF)
        mx = jnp.max(s, axis=-1, keepdims=True)
        va = jnp.concatenate([v.astype(BF16), ones], axis=-1)
        if init:
            m_ref[rows, :] = jnp.broadcast_to(mx, (span, LANES))
            pv = _dot(jnp.exp2(s - mx).astype(BF16), va)
            acc_ref[rows, :] = pv[:, :HEAD_DIM]
            l_ref[rows, :] = pv[:, HEAD_DIM:]
        else:
            m_old = m_ref[rows, :]
            m_new = jnp.maximum(m_old, mx)
            alpha = jnp.exp2(m_old - m_new)
            pv = _dot(jnp.exp2(s - _lane_tile(m_new, 2)).astype(BF16), va)
            m_ref[rows, :] = m_new
            acc_ref[rows, :] = alpha * acc_ref[rows, :] + pv[:, :HEAD_DIM]
            l_ref[rows, :] = alpha * l_ref[rows, :] + pv[:, HEAD_DIM:]

    for bi, (window, dil) in enumerate(DSW_PAIRS):
        assert window // dil == span
        n_sub = tile // (span * dil)
        prev0 = tile - span * dil
        for r in range(dil):
            def rows_of(start, n=span):
                return pl.ds(start, n, stride=dil) if dil > 1 else pl.ds(start, n)
            k = jnp.concatenate([kp_ref[rows_of(prev0 + r), :], kc_ref[rows_of(r), :]], axis=0)
            v = jnp.concatenate([vp_ref[rows_of(prev0 + r), :], vc_ref[rows_of(r), :]], axis=0)
            unit(q_ref[rows_of(r), :], k, v, band_first, rows_of(r), bi == 0)
            for i in range(1, n_sub):
                st = r + span * dil * i
                unit(q_ref[rows_of(st), :], kc_ref[rows_of(st - span * dil, 2 * span), :],
                     vc_ref[rows_of(st - span * dil, 2 * span), :], band, rows_of(st), bi == 0)
    o_ref[...] = acc_ref[...] / l_ref[...]


def _dsw(qk, proj, v_col):
    b, s, _ = qk.shape
    tile = DSW_TILE
    cur = lambda off: pl.BlockSpec((None, tile, HEAD_DIM), lambda bi, h, t: (bi, t, off + h))
    prev = lambda off: pl.BlockSpec((None, tile, HEAD_DIM), lambda bi, h, t: (bi, jnp.maximum(t - 1, 0), off + h))
    return pl.pallas_call(
        _dsw_kernel,
        grid=(b, DSW_HEADS, s // tile),
        in_specs=[cur(0), prev(DSW_HEADS), cur(DSW_HEADS), prev(v_col), cur(v_col)],
        out_specs=cur(0),
        out_shape=jax.ShapeDtypeStruct((b, s, DSW_HEADS * HEAD_DIM), F32),
        scratch_shapes=[pltpu.VMEM((tile, HEAD_DIM), F32)] * 3,
        compiler_params=_cparams("parallel", "parallel", "arbitrary"),
        name="dsw",
    )(qk, qk, qk, proj, proj)


def _nsa_compress_kernel(x_ref, pos_ref, w1_ref, w2_ref, o_ref, shift_ref):
    st = NSA_CMP_STRIDE
    n = x_ref.shape[0]
    p_acc = None
    q_acc = None
    for l in range(st):
        xl = x_ref[:, l * HEAD_DIM:(l + 1) * HEAD_DIM].astype(F32)
        pt = _dot((xl + pos_ref[l:l + 1, :]).astype(BF16), w1_ref[l])
        qt = _dot((xl + pos_ref[st + l:st + l + 1, :]).astype(BF16), w1_ref[st + l])
        p_acc = pt if p_acc is None else p_acc + pt
        q_acc = qt if q_acc is None else q_acc + qt
    shift_ref[0:n, :] = q_acc
    shift_ref[n:n + 8, :] = jnp.zeros((8, q_acc.shape[1]), F32)
    hid = p_acc + shift_ref[pl.ds(1, n), :]
    o_ref[...] = _dot(_silu(hid).astype(BF16), w2_ref[...]).astype(o_ref.dtype)


def _nsa_compress(src, pos, w1, w2):
    b, s, _ = src.shape
    st = NSA_CMP_STRIDE
    n = s // st
    hid = w1.shape[-1]
    x16 = src.reshape(b, n, st, NSA_KV_HEADS, HEAD_DIM).transpose(0, 3, 1, 2, 4).reshape(b, NSA_KV_HEADS, n, st * HEAD_DIM)
    return pl.pallas_call(
        _nsa_compress_kernel,
        grid=(b, NSA_KV_HEADS),
        in_specs=[pl.BlockSpec((None, None, n, st * HEAD_DIM), lambda bi, g: (bi, g, 0, 0)),
                  pl.BlockSpec((NSA_CMP_BLOCK, HEAD_DIM), lambda bi, g: (0, 0)),
                  pl.BlockSpec((NSA_CMP_BLOCK, HEAD_DIM, hid), lambda bi, g: (0, 0, 0)),
                  pl.BlockSpec((hid, HEAD_DIM), lambda bi, g: (0, 0))],
        out_specs=pl.BlockSpec((None, None, n, HEAD_DIM), lambda bi, g: (bi, g, 0, 0)),
        out_shape=jax.ShapeDtypeStruct((b, NSA_KV_HEADS, n, HEAD_DIM), BF16),
        scratch_shapes=[pltpu.VMEM((n + 8, hid), F32)],
        compiler_params=_cparams("parallel", "parallel"),
        name="nsa_compress",
    )(x16, pos, w1, w2)


NSA_ROWS = 64


def _stack_heads(q_ref, dst_ref, lanes=slice(0, HEAD_DIM)):
    tq = q_ref.shape[0]
    for h in range(NSA_HPG):
        dst_ref[h * tq:(h + 1) * tq, lanes] = q_ref[:, h * HEAD_DIM:(h + 1) * HEAD_DIM]


def _nsa_cmp_kernel(q_ref, kc_ref, vc_ref, ovt_ref, oc_ref, sel_ref, bias_ref, s_ref, p_ref, psum_ref):
    tq = q_ref.shape[0]
    ncp = kc_ref.shape[0]
    rb = NSA_ROWS
    t0 = pl.program_id(2) * tq
    tpos = t0 + lax.broadcasted_iota(jnp.int32, (tq, ncp), 0)
    cend = lax.broadcasted_iota(jnp.int32, (tq, ncp), 1) * NSA_CMP_STRIDE + (NSA_CMP_BLOCK - 1)
    bias_ref[...] = jnp.where(cend <= tpos, 0.0, NEG_INF)
    kc = kc_ref[...]
    vc = vc_ref[...]
    for h in range(NSA_HPG):
        sl = slice(h * HEAD_DIM, (h + 1) * HEAD_DIM)
        s_ref[h] = _dot_nt(q_ref[:, sl], kc)
        for rc in range(tq // rb):
            rows = pl.ds(rc * rb, rb)
            s = s_ref[h, rows, :] + bias_ref[rows, :]
            m = jnp.max(s, axis=-1, keepdims=True)
            e = jnp.exp2(s - m)
            inv = jnp.where(m > 0.5 * NEG_INF, 1.0 / jnp.sum(e, axis=-1, keepdims=True), 0.0)
            p = e * inv
            p_ref[h, rows, :] = p.astype(BF16)
            psum_ref[rows, :] = p if h == 0 else psum_ref[rows, :] + p
        oc_ref[:, sl] = _dot(p_ref[h], vc).astype(oc_ref.dtype)

    psum = psum_ref[...]
    p_hi = psum.astype(BF16)
    p_lo = (psum - p_hi.astype(F32)).astype(BF16)
    imp_t = _dot_nt(ovt_ref[...], p_hi) + _dot_nt(ovt_ref[...], p_lo)
    for cg in range(tq // LANES):
        jb = lax.broadcasted_iota(jnp.int32, (LANES, LANES), 0)
        tpos = t0 + cg * LANES + lax.broadcasted_iota(jnp.int32, (LANES, LANES), 1)
        cur = tpos // NSA_SLC_BLOCK
        forced = (jb == 0) | (jb == cur) | (jb == cur - 1)
        score = imp_t[:, cg * LANES:(cg + 1) * LANES] + jnp.where(forced, NSA_FORCE_BONUS, 0.0)
        score = jnp.where(jb <= cur, score, NEG_INF)
        taken = -3e38
        for _ in range(NSA_SLC_TOPK):
            top = jnp.max(score, axis=0, keepdims=True)
            idx = jnp.min(jnp.where(score == top, jb, LANES), axis=0, keepdims=True)
            score = jnp.where(jb == idx, taken, score)
        sel = jnp.where(score == taken, 0.0, NEG_INF)
        sel_ref[cg * LANES:(cg + 1) * LANES, :] = sel.T.astype(BF16)


NSA_KEY_TILE = 512


def _nsa_slc_kernel(q_ref, sel_ref, k_ref, v_ref, oh_ref, o_ref, qa_ref, s_ref, p_ref, acc_ref, m_ref):
    tq = q_ref.shape[0]
    tk = NSA_KEY_TILE
    rb = NSA_ROWS
    t0 = pl.program_id(2) * tq
    _stack_heads(q_ref, qa_ref)
    for h in range(NSA_HPG):
        qa_ref[h * tq:(h + 1) * tq, HEAD_DIM:] = sel_ref[...]
    acc_ref[...] = jnp.zeros_like(acc_ref)
    m_ref[...] = jnp.full(m_ref.shape, NEG_INF, F32)
    ones = jnp.ones((tk, HEAD_DIM), BF16)

    def tile(jt, on_diagonal):
        k0 = pl.multiple_of(jt * tk, tk)
        ka = jnp.concatenate([k_ref[pl.ds(k0, tk), :], oh_ref[pl.ds(k0, tk), :]], axis=-1)
        va = jnp.concatenate([v_ref[pl.ds(k0, tk), :], ones], axis=-1)
        for h in range(NSA_HPG):
            s_ref[h] = _dot_nt(qa_ref[h * tq:(h + 1) * tq, :], ka)
            for rc in range(tq // rb):
                rows = pl.ds(rc * rb, rb)
                s = s_ref[h, rows, :]
                if on_diagonal:
                    tpos = t0 + rc * rb + lax.broadcasted_iota(jnp.int32, (rb, tk), 0)
                    s = jnp.where(k0 + lax.broadcasted_iota(jnp.int32, (rb, tk), 1) <= tpos, s, NEG_INF)
                m_old = m_ref[h, rows, :]
                m_new = jnp.maximum(m_old, jnp.max(s, axis=-1, keepdims=True))
                p = jnp.exp2(s - _lane_tile(m_new, tk // LANES))
                m_ref[h, rows, :] = m_new
                acc_ref[h, rows, :] = _lane_tile(jnp.exp2(m_old - m_new), 2) * acc_ref[h, rows, :]
                p_ref[h, rows, :] = p.astype(BF16)
            acc_ref[h] += _dot(p_ref[h], va)

    n_below = t0 // tk

    def pair(jp, carry):
        tile(2 * jp, False)
        tile(2 * jp + 1, False)
        return carry

    lax.fori_loop(0, n_below // 2, pair, 0)

    @pl.when(n_below % 2 == 1)
    def _():
        tile(n_below - 1, False)

    tile(n_below, True)
    for h in range(NSA_HPG):
        o_ref[:, h * HEAD_DIM:(h + 1) * HEAD_DIM] = (acc_ref[h, :, :HEAD_DIM] / acc_ref[h, :, HEAD_DIM:]).astype(o_ref.dtype)


def _nsa_win_kernel(q_ref, k_ref, v_ref, o_ref, bias_ref, s_ref, p_ref):
    tq = q_ref.shape[0]
    nk = NSA_WINDOW + tq
    rb = NSA_ROWS
    t0 = pl.program_id(2) * tq
    start = pl.multiple_of(jnp.maximum(t0 - NSA_WINDOW, 0), tq)
    tpos = t0 + lax.broadcasted_iota(jnp.int32, (tq, nk), 0)
    kpos = start + lax.broadcasted_iota(jnp.int32, (tq, nk), 1)
    bias_ref[...] = jnp.where((kpos <= tpos) & (kpos > tpos - NSA_WINDOW), 0.0, NEG_INF)
    kt = k_ref[pl.ds(start, nk), :]
    va = jnp.concatenate([v_ref[pl.ds(start, nk), :], jnp.ones((nk, HEAD_DIM), BF16)], axis=-1)
    for h in range(NSA_HPG):
        sl = slice(h * HEAD_DIM, (h + 1) * HEAD_DIM)
        s_ref[h] = _dot_nt(q_ref[:, sl], kt)
        for rc in range(tq // rb):
            rows = pl.ds(rc * rb, rb)
            s = s_ref[h, rows, :] + bias_ref[rows, :]
            p_ref[h, rows, :] = jnp.exp2(s - jnp.max(s, axis=-1, keepdims=True)).astype(BF16)
        pv = _dot(p_ref[h], va)
        o_ref[:, sl] = (pv[:, :HEAD_DIM] / pv[:, HEAD_DIM:]).astype(o_ref.dtype)


def _nsa_overlap_t(n_cmp_pad, n_cmp):
    c_start = np.arange(n_cmp_pad)[None, :] * NSA_CMP_STRIDE
    s_start = np.arange(LANES)[:, None] * NSA_SLC_BLOCK
    ov = (c_start < s_start + NSA_SLC_BLOCK) & (c_start + NSA_CMP_BLOCK > s_start)
    ov &= (np.arange(n_cmp_pad) < n_cmp)[None, :]
    return jnp.asarray(ov, BF16)


def _nsa_attention(hd, pv, kc, vc, tq_cmp=512, tq_slc=512, tq_win=256):
    b, s, _ = hd.shape
    g = NSA_KV_HEADS
    qw = NSA_HPG * HEAD_DIM
    ncp = s // NSA_CMP_STRIDE
    nk_win = NSA_WINDOW + tq_win
    assert s // NSA_SLC_BLOCK <= LANES and s >= nk_win and NSA_KEY_TILE % tq_slc == 0
    grid = lambda tq: (b, g, s // tq)
    q_spec = lambda tq: pl.BlockSpec((None, tq, qw), lambda bi, gi, i: (bi, i, gi))
    seq_spec = lambda off: pl.BlockSpec((None, s, HEAD_DIM), lambda bi, gi, i: (bi, 0, off + gi))
    cmp_spec = pl.BlockSpec((None, None, ncp, HEAD_DIM), lambda bi, gi, i: (bi, gi, 0, 0))
    sel_spec = lambda tq: pl.BlockSpec((None, None, tq, LANES), lambda bi, gi, i: (bi, gi, i, 0))
    const_spec = lambda shape: pl.BlockSpec(shape, lambda bi, gi, i: (0, 0))
    out_sd = jax.ShapeDtypeStruct((b, s, NSA_HEADS * HEAD_DIM), BF16)
    sel_sd = jax.ShapeDtypeStruct((b, g, s, LANES), BF16)
    sem = ("parallel", "parallel", "arbitrary")
    k_slc, k_win = NSA_HEADS + g, NSA_HEADS + 2 * g
    v_slc, v_win = g, 2 * g
    block_onehot = jnp.asarray(np.arange(s)[:, None] // NSA_SLC_BLOCK == np.arange(LANES)[None, :], BF16)

    rows = NSA_HPG * tq_cmp
    o_c, sel = pl.pallas_call(
        _nsa_cmp_kernel, grid=grid(tq_cmp),
        in_specs=[q_spec(tq_cmp), cmp_spec, cmp_spec, const_spec((LANES, ncp))],
        out_specs=[q_spec(tq_cmp), sel_spec(tq_cmp)], out_shape=[out_sd, sel_sd],
        scratch_shapes=[pltpu.VMEM((tq_cmp, ncp), F32), pltpu.VMEM((NSA_HPG, tq_cmp, ncp), F32),
                        pltpu.VMEM((NSA_HPG, tq_cmp, ncp), BF16), pltpu.VMEM((tq_cmp, ncp), F32)],
        compiler_params=_cparams(*sem), name="nsa_cmp",
    )(hd, kc, vc, _nsa_overlap_t(ncp, ncp - 1))
    tq = tq_slc
    o_s = pl.pallas_call(
        _nsa_slc_kernel, grid=grid(tq),
        in_specs=[q_spec(tq), sel_spec(tq), seq_spec(k_slc), seq_spec(v_slc), const_spec((s, LANES))],
        out_specs=q_spec(tq), out_shape=out_sd,
        scratch_shapes=[pltpu.VMEM((NSA_HPG * tq, 2 * HEAD_DIM), BF16), pltpu.VMEM((NSA_HPG, tq, NSA_KEY_TILE), F32),
                        pltpu.VMEM((NSA_HPG, tq, NSA_KEY_TILE), BF16), pltpu.VMEM((NSA_HPG, tq, 2 * HEAD_DIM), F32),
                        pltpu.VMEM((NSA_HPG, tq, HEAD_DIM), F32)],
        compiler_params=_cparams(*sem), name="nsa_slc",
    )(hd, sel, hd, pv, block_onehot)
    rows = NSA_HPG * tq_win
    o_w = pl.pallas_call(
        _nsa_win_kernel, grid=grid(tq_win),
        in_specs=[q_spec(tq_win), seq_spec(k_win), seq_spec(v_win)],
        out_specs=q_spec(tq_win), out_shape=out_sd,
        scratch_shapes=[pltpu.VMEM((tq_win, nk_win), F32), pltpu.VMEM((NSA_HPG, tq_win, nk_win), F32),
                        pltpu.VMEM((NSA_HPG, tq_win, nk_win), BF16)],
        compiler_params=_cparams(*sem), name="nsa_win",
    )(hd, hd, pv)
    return o_c, o_s, o_w


def _row(v, width=None):
    v = v.reshape(1, -1).astype(F32)
    if width is not None and v.shape[1] < width:
        v = jnp.pad(v, ((0, 0), (0, width - v.shape[1])))
    return v


def _even_mixer(xf, b, s, gain, w_in, w_out, conv_w, a_log, dt_bias, gdn_norm, q_norm, k_norm, rope):
    gw = GDN_HEADS * HEAD_DIM
    dw = DSW_HEADS * HEAD_DIM
    o_small = 4 * gw
    o_dsw = o_small + 2 * GDN_HEADS
    w_plain = jnp.concatenate([w_in[:, :o_small], w_in[:, o_dsw + 2 * dw:]], axis=1).astype(BF16)
    w_heads = w_in[:, o_dsw:o_dsw + 2 * dw].astype(BF16)
    w_small = jnp.pad(w_in[:, o_small:o_dsw], ((0, 0), (0, LANES - 2 * GDN_HEADS))).astype(BF16)
    pad8 = lambda v: jnp.pad(_row(v), ((0, 0), (GDN_HEADS, LANES - 2 * GDN_HEADS)))
    head_gain = jnp.concatenate([jnp.tile(_row(q_norm) * Q_SCALE_LOG2, (1, DSW_HEADS)),
                                 jnp.tile(_row(k_norm), (1, DSW_HEADS))], axis=1)

    proj = _norm_proj(xf, gain, w_plain, tn=1024)
    qk, small = _norm_proj_heads(xf, gain, w_heads, head_gain, rope, F32, w_small, "gdn_gates",
                                 (pad8(a_log), pad8(dt_bias)))
    cw = conv_w.reshape(GDN_CONV, 3, gw).transpose(1, 0, 2)
    o_a = _gdn(proj.reshape(b, s, -1), small.reshape(b, s, LANES), cw, _row(gdn_norm))
    o_b = _dsw(qk.reshape(b, s, -1), proj.reshape(b, s, -1), o_small // HEAD_DIM)
    t = b * s
    return _proj_res(xf, [o_a.reshape(t, gw), o_b.reshape(t, dw)],
                     [w_out[:gw].astype(BF16), w_out[gw:].astype(BF16)])


def _odd_mixer(xf, b, s, gain, w_in, w_out, q_norm, k_norm, cmp_pos, cmp_w1, cmp_w2, rope):
    qw = NSA_HEADS * HEAD_DIM
    kvw = NSA_KV_HEADS * HEAD_DIM
    col = lambda i: slice(qw + i * kvw, qw + (i + 1) * kvw)
    w_heads = jnp.concatenate([w_in[:, :qw], w_in[:, col(0)], w_in[:, col(2)], w_in[:, col(4)]], axis=1).astype(BF16)
    w_plain = jnp.concatenate([w_in[:, col(1)], w_in[:, col(3)], w_in[:, col(5)]], axis=1).astype(BF16)
    w_gate = jnp.pad(w_in[:, qw + 6 * kvw:], ((0, 0), (0, LANES - 3 * NSA_HEADS))).astype(BF16)
    head_gain = jnp.concatenate([jnp.tile(_row(q_norm) * Q_SCALE_LOG2, (1, NSA_HEADS))]
                                + [jnp.tile(_row(k_norm[i]), (1, NSA_KV_HEADS)) for i in range(3)], axis=1)

    hd, gates = _norm_proj_heads(xf, gain, w_heads, head_gain, rope, BF16, w_gate, "sigmoid")
    hd = hd.reshape(b, s, -1)
    pv = _norm_proj(xf, gain, w_plain, out_dtype=BF16).reshape(b, s, -1)
    hid = cmp_w1.shape[-1]
    w1 = cmp_w1.reshape(2, NSA_CMP_BLOCK, HEAD_DIM, hid).astype(BF16)
    kc = _nsa_compress(hd[:, :, qw:qw + kvw], _perm_head_dims(cmp_pos[0]), _perm_head_dims(w1[0], axis=1),
                       _perm_head_dims(cmp_w2[0].astype(BF16)))
    vc = _nsa_compress(pv[:, :, :kvw], cmp_pos[1], w1[1], cmp_w2[1].astype(BF16))
    o_c, o_s, o_w = _nsa_attention(hd, pv, kc, vc)
    t = b * s
    return _proj_res(xf, [o_c.reshape(t, qw), o_s.reshape(t, qw), o_w.reshape(t, qw)], [w_out.astype(BF16)], gates)


def kernel(x, mem, positions, ffn1_norm, ffn1_w_gu, ffn1_w_down, mix_norm, ev_w_in, ev_w_out, gdn_conv_w, gdn_a_log, gdn_dt_bias, gdn_out_norm, dsw_q_norm, dsw_k_norm, od_w_in, od_w_out, nsa_q_norm, nsa_k_norm, nsa_cmp_pos, nsa_cmp_w1, nsa_cmp_w2, xa_norm, xa_mem_norm, xa_w_q, xa_w_kv, xa_q_norm, xa_k_norm, xa_w_o, ffn2_norm, ffn2_w_gu, ffn2_w_down):
    b, s, d = x.shape
    depth = ffn1_norm.shape[0]
    rope = _rope_tables(positions)
    xf = x.reshape(b * s, d)
    for i in range(depth):
        xf = _ffn(xf, _row(ffn1_norm[i]), ffn1_w_gu[i].astype(BF16), ffn1_w_down[i].astype(BF16))
        if i % 2 == 0:
            e = i // 2
            xf = _even_mixer(xf, b, s, _row(mix_norm[i]), ev_w_in[e], ev_w_out[e], gdn_conv_w[e], gdn_a_log[e],
                             gdn_dt_bias[e], gdn_out_norm[e], dsw_q_norm[e], dsw_k_norm[e], rope)
        else:
            o = i // 2
            xf = _odd_mixer(xf, b, s, _row(mix_norm[i]), od_w_in[o], od_w_out[o], nsa_q_norm[o], nsa_k_norm[o],
                            nsa_cmp_pos[o], nsa_cmp_w1[o], nsa_cmp_w2[o], rope)
        xf = _cross_attention(xf.reshape(b, s, d), mem, _row(xa_norm[i]), _row(xa_mem_norm[i]),
                              xa_w_q[i].astype(BF16), xa_w_kv[i].astype(BF16), _row(xa_q_norm[i]),
                              _row(xa_k_norm[i]), xa_w_o[i].astype(BF16)).reshape(b * s, d)
        xf = _ffn(xf, _row(ffn2_norm[i]), ffn2_w_gu[i].astype(BF16), ffn2_w_down[i].astype(BF16))
    return xf.reshape(b, s, d)
```

```python
import functools

import numpy as np
import jax
import jax.numpy as jnp
from jax import lax
from jax.experimental import pallas as pl
from jax.experimental.pallas import tpu as pltpu

F32 = jnp.float32
BF16 = jnp.bfloat16
HI = lax.Precision.HIGHEST

EPS = 1e-6
NEG_INF = -1e30
HEAD_DIM = 128
ROPE_THETA = 500000.0
ROPE_DIM = HEAD_DIM // 4
ATT_SCALE = HEAD_DIM ** -0.5
LOG2E = float(np.log2(np.e))
Q_SCALE_LOG2 = ATT_SCALE * LOG2E

GDN_HEADS = 8
GDN_CHUNK = 64
GDN_CONV = 4
DSW_HEADS = 8
DSW_PAIRS = ((128, 1), (512, 4), (2048, 16))
DSW_SPAN = 128
DSW_TILE = 2048
NSA_HEADS = 16
NSA_KV_HEADS = 4
NSA_HPG = NSA_HEADS // NSA_KV_HEADS
NSA_CMP_BLOCK = 32
NSA_CMP_STRIDE = 16
NSA_SLC_BLOCK = 64
NSA_SLC_TOPK = 16
NSA_WINDOW = 512
NSA_FORCE_BONUS = 1e3
XA_HEADS = 4

VMEM_LIMIT_BYTES = 56 * 1024 * 1024
LANES = 128


def _cparams(*sem):
    return pltpu.CompilerParams(dimension_semantics=sem, vmem_limit_bytes=VMEM_LIMIT_BYTES)


def _dot(a, b, precision=None):
    return jnp.dot(a, b, preferred_element_type=F32, precision=precision)


def _dot_nt(a, b, precision=None):
    return lax.dot_general(a, b, (((1,), (1,)), ((), ())), preferred_element_type=F32, precision=precision)


def _dot_tn(a, b, precision=None):
    return lax.dot_general(a, b, (((0,), (0,)), ((), ())), preferred_element_type=F32, precision=precision)


def _bmm(a, b, dn):
    return lax.dot_general(a.astype(BF16), b.astype(BF16), dn, preferred_element_type=F32)


def _bdot(a, b):
    return _bmm(a, b, (((2,), (1,)), ((0,), (0,))))


def _bdot_nt(a, b):
    return _bmm(a, b, (((2,), (2,)), ((0,), (0,))))


def _bdot_tn(a, b):
    return _bmm(a, b, (((1,), (1,)), ((0,), (0,))))


def _rms(x, gain):
    return x * lax.rsqrt(jnp.mean(x * x, axis=-1, keepdims=True) + EPS) * gain


def _silu(x):
    return x * jax.nn.sigmoid(x)


def _lane_tile(x, n):
    return jnp.concatenate([x] * n, axis=1)


def _ffn_kernel(x_ref, g_ref, wg_ref, wu_ref, wd_ref, o_ref, xn_ref, acc_ref):
    f = pl.program_id(1)

    @pl.when(f == 0)
    def _():
        xn_ref[...] = _rms(x_ref[...], g_ref[...]).astype(BF16)
        acc_ref[...] = jnp.zeros_like(acc_ref)

    xn = xn_ref[...]
    gate = _dot(xn, wg_ref[...])
    up = _dot(xn, wu_ref[...])
    act = (_silu(gate) * up).astype(BF16)
    acc_ref[...] += _dot(act, wd_ref[...])

    @pl.when(f == pl.num_programs(1) - 1)
    def _():
        o_ref[...] = x_ref[...] + 0.5 * acc_ref[...]


def _ffn(x, gain, w_gu, w_down, tm=512, tf=512):
    t, d = x.shape
    d_ff = w_down.shape[0]
    n_f = d_ff // tf
    return pl.pallas_call(
        _ffn_kernel,
        grid=(t // tm, n_f),
        in_specs=[
            pl.BlockSpec((tm, d), lambda i, f: (i, 0)),
            pl.BlockSpec((1, d), lambda i, f: (0, 0)),
            pl.BlockSpec((d, tf), lambda i, f: (0, f)),
            pl.BlockSpec((d, tf), lambda i, f: (0, f + n_f)),
            pl.BlockSpec((tf, d), lambda i, f: (f, 0)),
        ],
        out_specs=pl.BlockSpec((tm, d), lambda i, f: (i, 0)),
        out_shape=jax.ShapeDtypeStruct((t, d), F32),
        scratch_shapes=[pltpu.VMEM((tm, d), BF16), pltpu.VMEM((tm, d), F32)],
        compiler_params=_cparams("parallel", "arbitrary"),
        name="ffn",
    )(x, gain, w_gu, w_gu, w_down)


ROPE_HALF = ROPE_DIM // 2
ROPE_SHIFT = LANES // 2
ROPE_PERM = np.arange(HEAD_DIM)
ROPE_PERM[ROPE_HALF:ROPE_DIM] = np.arange(ROPE_SHIFT, ROPE_SHIFT + ROPE_HALF)
ROPE_PERM[ROPE_SHIFT:ROPE_SHIFT + ROPE_HALF] = np.arange(ROPE_HALF, ROPE_DIM)


def _perm_head_dims(a, axis=-1):
    n_heads = a.shape[axis] // HEAD_DIM
    idx = (np.arange(n_heads)[:, None] * HEAD_DIM + ROPE_PERM[None, :]).reshape(-1)
    return jnp.take(a, idx, axis=axis)


def _rope_kernel(pos_ref, invf_ref, c_ref, s_ref):
    ang = pos_ref[...] * invf_ref[...]
    lane = lax.broadcasted_iota(jnp.int32, ang.shape, 1)
    sin = jnp.sin(ang)
    c_ref[...] = jnp.cos(ang)
    s_ref[...] = jnp.where(lane < ROPE_SHIFT, -sin, sin)


def _rope_tables(positions, tm=1024):
    t = positions.size
    inv_freq = jnp.float32(ROPE_THETA) ** (-jnp.arange(ROPE_HALF, dtype=F32) / ROPE_HALF)
    invf = jnp.zeros((1, LANES), F32).at[0, :ROPE_HALF].set(inv_freq).at[0, ROPE_SHIFT:ROPE_SHIFT + ROPE_HALF].set(inv_freq)
    pos = positions.astype(F32).reshape(t, 1)
    spec = pl.BlockSpec((tm, LANES), lambda i: (i, 0))
    return pl.pallas_call(
        _rope_kernel,
        grid=(t // tm,),
        in_specs=[pl.BlockSpec((tm, 1), lambda i: (i, 0)), pl.BlockSpec((1, LANES), lambda i: (0, 0))],
        out_specs=[spec, spec],
        out_shape=[jax.ShapeDtypeStruct((t, LANES), F32)] * 2,
        compiler_params=_cparams("parallel"),
        name="rope_tables",
    )(pos, invf)


def _rope(x, c, s):
    return x * c + pltpu.roll(x, ROPE_SHIFT, 1) * s


def _norm_proj_kernel(x_ref, g_ref, w_ref, o_ref, xn_ref):
    @pl.when(pl.program_id(1) == 0)
    def _():
        xn_ref[...] = _rms(x_ref[...], g_ref[...]).astype(BF16)

    o_ref[...] = _dot(xn_ref[...], w_ref[...]).astype(o_ref.dtype)


def _norm_proj(x, gain, w, out_dtype=F32, tm=1024, tn=512):
    t, d = x.shape
    n = w.shape[1]
    tn = min(tn, n)
    assert t % tm == 0 and n % tn == 0
    return pl.pallas_call(
        _norm_proj_kernel,
        grid=(t // tm, n // tn),
        in_specs=[pl.BlockSpec((tm, d), lambda i, j: (i, 0)),
                  pl.BlockSpec((1, d), lambda i, j: (0, 0)),
                  pl.BlockSpec((d, tn), lambda i, j: (0, j))],
        out_specs=pl.BlockSpec((tm, tn), lambda i, j: (i, j)),
        out_shape=jax.ShapeDtypeStruct((t, n), out_dtype),
        scratch_shapes=[pltpu.VMEM((tm, d), BF16)],
        compiler_params=_cparams("parallel", "arbitrary"),
        name="norm_proj_plain",
    )(x, gain, w)


def _norm_proj_heads_kernel(gate_mode, x_ref, g_ref, w_ref, hg_ref, c_ref, s_ref, wg_ref, *rest):
    o_ref, og_ref = rest[-2], rest[-1]
    xn = _rms(x_ref[...], g_ref[...]).astype(BF16)
    c, s = c_ref[...], s_ref[...]
    pair = 2 * HEAD_DIM
    for kp in range(w_ref.shape[1] // pair):
        y = _dot(xn, w_ref[:, kp * pair:(kp + 1) * pair])
        for k in range(2):
            sl = slice(kp * pair + k * HEAD_DIM, kp * pair + (k + 1) * HEAD_DIM)
            yk = y[:, k * HEAD_DIM:(k + 1) * HEAD_DIM]
            o_ref[:, sl] = _rope(_rms(yk, hg_ref[:, sl]), c, s).astype(o_ref.dtype)
    y = _dot(xn, wg_ref[...])
    if gate_mode == "sigmoid":
        og_ref[...] = jax.nn.sigmoid(y)
    else:
        alog_ref, dt_ref = rest[0], rest[1]
        lane = lax.broadcasted_iota(jnp.int32, y.shape, 1)
        z = y + dt_ref[...]
        softplus = jnp.maximum(z, 0.0) + jnp.log1p(jnp.exp(-jnp.abs(z)))
        og_ref[...] = jnp.where(lane < GDN_HEADS, jax.nn.sigmoid(y), -jnp.exp(alog_ref[...]) * softplus)


def _norm_proj_heads(x, gain, w, head_gain, rope, out_dtype, w_gate, gate_mode, gate_rows=(), tm=512):
    t, d = x.shape
    n = w.shape[1]
    assert t % tm == 0 and n % (2 * HEAD_DIM) == 0 and w_gate.shape == (d, LANES)
    assert gate_mode in ("sigmoid", "gdn_gates")
    w = _perm_head_dims(w)
    head_gain = _perm_head_dims(head_gain)
    full = lambda shape: pl.BlockSpec(shape, lambda i: (0, 0))
    row = lambda width: pl.BlockSpec((tm, width), lambda i: (i, 0))
    return pl.pallas_call(
        functools.partial(_norm_proj_heads_kernel, gate_mode),
        grid=(t // tm,),
        in_specs=[row(d), full((1, d)), full((d, n)), full((1, n)), row(LANES), row(LANES), full((d, LANES))]
                 + [full((1, LANES))] * len(gate_rows),
        out_specs=[row(n), row(LANES)],
        out_shape=[jax.ShapeDtypeStruct((t, n), out_dtype), jax.ShapeDtypeStruct((t, LANES), F32)],
        compiler_params=_cparams("parallel"),
        name="norm_proj_heads",
    )(x, gain, w, head_gain, *rope, w_gate, *gate_rows)


def _proj_res_kernel(n_in, gated, *refs):
    x_ref = refs[0]
    a_refs = refs[1:1 + n_in]
    o_ref = refs[-1]
    w_refs = refs[1 + n_in:-2] if gated else refs[1 + n_in:-1]
    if gated:
        gt = refs[-2][...].astype(BF16)
        width = a_refs[0].shape[1]
        src = lax.broadcasted_iota(jnp.int32, (LANES, width), 0)
        head = lax.broadcasted_iota(jnp.int32, (LANES, width), 1) // HEAD_DIM
        acc = None
        for j in range(n_in):
            spread = jnp.where(src == 3 * head + j, 1.0, 0.0).astype(BF16)
            term = _dot(gt, spread) * a_refs[j][...]
            acc = term if acc is None else acc + term
        y = _dot(acc.astype(BF16), w_refs[0][...])
    else:
        y = None
        for a_ref, w_ref in zip(a_refs, w_refs):
            term = _dot(a_ref[...].astype(BF16), w_ref[...])
            y = term if y is None else y + term
    o_ref[...] = x_ref[...] + y


def _proj_res(x, acts, ws, gates=None, tm=512):
    t, d = x.shape
    n_in = len(acts)
    row = lambda i: (i, 0)
    in_specs = [pl.BlockSpec((tm, d), row)]
    in_specs += [pl.BlockSpec((tm, a.shape[1]), row) for a in acts]
    in_specs += [pl.BlockSpec(w.shape, lambda i: (0, 0)) for w in ws]
    args = [x, *acts, *ws]
    if gates is not None:
        in_specs.append(pl.BlockSpec((tm, LANES), row))
        args.append(gates)
    return pl.pallas_call(
        functools.partial(_proj_res_kernel, n_in, gates is not None),
        grid=(t // tm,),
        in_specs=in_specs,
        out_specs=pl.BlockSpec((tm, d), row),
        out_shape=jax.ShapeDtypeStruct((t, d), F32),
        compiler_params=_cparams("parallel"),
        name="proj_res",
    )(*args)


def _xa_kv_kernel(mem_ref, gm_ref, w_ref, kn_ref, k_ref, v_ref):
    kv = _dot(_rms(mem_ref[...], gm_ref[...]).astype(BF16), w_ref[...])
    xa_w = XA_HEADS * HEAD_DIM
    for h in range(XA_HEADS):
        sl = slice(h * HEAD_DIM, (h + 1) * HEAD_DIM)
        k_ref[:, sl] = _rms(kv[:, sl], kn_ref[...]).astype(BF16)
    v_ref[...] = kv[:, xa_w:].astype(BF16)


def _xa_kernel(x_ref, g_ref, wq_ref, qn_ref, k_ref, v_ref, wo_ref, o_ref):
    x = x_ref[...]
    q = _dot(_rms(x, g_ref[...]).astype(BF16), wq_ref[...])
    outs = []
    for h in range(XA_HEADS):
        sl = slice(h * HEAD_DIM, (h + 1) * HEAD_DIM)
        qh = (_rms(q[:, sl], qn_ref[...]) * Q_SCALE_LOG2).astype(BF16)
        s = _dot_nt(qh, k_ref[:, sl])
        p = jnp.exp2(s - jnp.max(s, axis=-1, keepdims=True))
        l = jnp.sum(p, axis=-1, keepdims=True)
        outs.append((_dot(p.astype(BF16), v_ref[:, sl]) / l).astype(BF16))
    o_ref[...] = x + _dot(jnp.concatenate(outs, axis=-1), wo_ref[...])


def _cross_attention(x, mem, g_x, g_mem, w_q, w_kv, q_norm, k_norm, w_o, tm=512):
    b, s, d = x.shape
    m_len = mem.shape[1]
    xa_w = XA_HEADS * HEAD_DIM
    full = lambda shape: pl.BlockSpec(shape, lambda *_: (0,) * len(shape))
    k, v = pl.pallas_call(
        _xa_kv_kernel,
        grid=(b,),
        in_specs=[pl.BlockSpec((None, m_len, d), lambda i: (i, 0, 0)), full((1, d)), full((d, 2 * xa_w)),
                  full((1, HEAD_DIM))],
        out_specs=[pl.BlockSpec((None, m_len, xa_w), lambda i: (i, 0, 0))] * 2,
        out_shape=[jax.ShapeDtypeStruct((b, m_len, xa_w), BF16)] * 2,
        compiler_params=_cparams("parallel"),
        name="xa_kv",
    )(mem, g_mem, w_kv, k_norm)
    return pl.pallas_call(
        _xa_kernel,
        grid=(b, s // tm),
        in_specs=[pl.BlockSpec((None, tm, d), lambda i, j: (i, j, 0)), full((1, d)), full((d, xa_w)),
                  full((1, HEAD_DIM)),
                  pl.BlockSpec((None, m_len, xa_w), lambda i, j: (i, 0, 0)),
                  pl.BlockSpec((None, m_len, xa_w), lambda i, j: (i, 0, 0)),
                  full((xa_w, d))],
        out_specs=pl.BlockSpec((None, tm, d), lambda i, j: (i, j, 0)),
        out_shape=jax.ShapeDtypeStruct((b, s, d), F32),
        compiler_params=_cparams("parallel", "parallel"),
        name="cross_attn",
    )(x, g_x, w_q, q_norm, k, v, w_o)


def _gdn_kernel(q_ref, k_ref, v_ref, z_ref, sm_ref, cw_ref, gn_ref, o_ref, halo_ref, xbuf_ref, qkv_ref, state_ref):
    c = GDN_CHUNK
    rows = q_ref.shape[0]
    n_ck = rows // c

    @pl.when(pl.program_id(1) == 0)
    def _():
        halo_ref[...] = jnp.zeros_like(halo_ref)
        state_ref[...] = jnp.zeros_like(state_ref)

    for idx, src in enumerate((q_ref, k_ref, v_ref)):
        xbuf_ref[0:8, :] = halo_ref[idx]
        xbuf_ref[8:8 + rows, :] = src[...]
        halo_ref[idx] = src[rows - 8:rows, :]
        w = cw_ref[idx]
        acc = xbuf_ref[8:8 + rows, :] * w[GDN_CONV - 1:GDN_CONV, :]
        for tap in range(GDN_CONV - 1):
            off = 8 - (GDN_CONV - 1) + tap
            acc = acc + xbuf_ref[off:off + rows, :] * w[tap:tap + 1, :]
        qkv_ref[idx] = _silu(acc)

    row = lax.broadcasted_iota(jnp.int32, (c, c), 0)
    col = lax.broadcasted_iota(jnp.int32, (c, c), 1)
    causal = row >= col
    strict = row > col
    eye_f = (row == col).astype(F32)
    tril_f = causal.astype(F32)
    triu_f = (col >= row).astype(F32)
    nh = GDN_HEADS
    csl = lambda ck: slice(ck * c, (ck + 1) * c)
    hsl = lambda h: slice(h * HEAD_DIM, (h + 1) * HEAD_DIM)
    sm = sm_ref[...]
    gam_cols = [_dot(tril_f, sm[csl(ck)], HI) for ck in range(n_ck)]
    gam_rows = [_dot_tn(sm[csl(ck)], triu_f, HI) for ck in range(n_ck)]

    pairs = [(ck, h) for ck in range(n_ck) for h in range(nh)]
    q = jnp.stack([qkv_ref[0, csl(ck), hsl(h)] for ck, h in pairs])
    k = jnp.stack([qkv_ref[1, csl(ck), hsl(h)] for ck, h in pairs])
    v = jnp.stack([qkv_ref[2, csl(ck), hsl(h)] for ck, h in pairs])
    beta = jnp.stack([sm[csl(ck), h:h + 1] for ck, h in pairs])
    gam = jnp.stack([gam_cols[ck][:, nh + h:nh + h + 1] for ck, h in pairs])
    gam_r = jnp.stack([gam_rows[ck][nh + h:nh + h + 1, :] for ck, h in pairs])
    q = q * lax.rsqrt(jnp.sum(q * q, axis=-1, keepdims=True) + EPS) * ATT_SCALE
    k = k * lax.rsqrt(jnp.sum(k * k, axis=-1, keepdims=True) + EPS)
    gam_last = gam[:, c - 1:c, :]
    eg = jnp.exp(gam)
    decay = jnp.where(causal, jnp.exp(jnp.where(causal, gam - gam_r, 0.0)), 0.0)
    kb = k * beta
    a_mat = jnp.where(strict, _bdot_nt(kb, k) * decay, 0.0)
    pw = -a_mat
    inv = eye_f + pw
    for _ in range(5):
        pw = _bdot(pw, pw)
        inv = inv + _bdot(inv, pw)
    uw = _bdot(inv, jnp.concatenate([v * beta, kb * eg], axis=-1))
    u, w = uw[..., :HEAD_DIM], uw[..., HEAD_DIM:]
    qk = jnp.where(causal, _bdot_nt(q, k) * decay, 0.0)
    q_dec = q * eg
    k_dec = k * jnp.exp(gam_last - gam)
    chunk_decay = jnp.exp(gam_last)
    state = state_ref[...]
    for ck in range(n_ck):
        bs = slice(ck * nh, (ck + 1) * nh)
        v_new = u[bs] - _bdot(w[bs], state)
        out = _rms(_bdot(q_dec[bs], state) + _bdot(qk[bs], v_new), gn_ref[...])
        state = state * chunk_decay[bs] + _bdot_tn(k_dec[bs], v_new)
        for h in range(nh):
            o_ref[csl(ck), hsl(h)] = out[h] * _silu(z_ref[csl(ck), hsl(h)])
    state_ref[...] = state


GDN_STEP_CHUNKS = 4


def _gdn(proj, small, conv_w, out_norm):
    b, s, _ = proj.shape
    c = GDN_CHUNK * GDN_STEP_CHUNKS
    assert s % c == 0
    w_h = GDN_HEADS * HEAD_DIM
    blk = lambda j: pl.BlockSpec((None, c, w_h), lambda bi, i: (bi, i, j))
    return pl.pallas_call(
        _gdn_kernel,
        grid=(b, s // c),
        in_specs=[blk(0), blk(1), blk(2), blk(3),
                  pl.BlockSpec((None, c, LANES), lambda bi, i: (bi, i, 0)),
                  pl.BlockSpec((3, GDN_CONV, w_h), lambda bi, i: (0, 0, 0)),
                  pl.BlockSpec((1, HEAD_DIM), lambda bi, i: (0, 0))],
        out_specs=pl.BlockSpec((None, c, w_h), lambda bi, i: (bi, i, 0)),
        out_shape=jax.ShapeDtypeStruct((b, s, w_h), F32),
        scratch_shapes=[pltpu.VMEM((3, 8, w_h), F32), pltpu.VMEM((c + 8, w_h), F32),
                        pltpu.VMEM((3, c, w_h), F32), pltpu.VMEM((GDN_HEADS, HEAD_DIM, HEAD_DIM), F32)],
        compiler_params=_cparams("parallel", "arbitrary"),
        name="gdn",
    )(proj, proj, proj, proj, small, conv_w, out_norm)


def _dsw_kernel(q_ref, kp_ref, kc_ref, vp_ref, vc_ref, o_ref, acc_ref, m_ref, l_ref):
    span = DSW_SPAN
    tile = DSW_TILE
    not_first = pl.program_id(2) > 0
    ri = lax.broadcasted_iota(jnp.int32, (span, 2 * span), 0)
    ci = lax.broadcasted_iota(jnp.int32, (span, 2 * span), 1)
    band = (ci >= ri) & (ci <= ri + span)
    band_first = band & ((ci >= span) | not_first)

    ones = jnp.ones((2 * span, HEAD_DIM), BF16)

    def unit(q, k, v, mask, rows, init):
        s = _dot_nt(q.astype(BF16), k.astype(BF16))
        s = jnp.where(mask, s, NEG_INF)
        mx = jnp.max(s, axis=-1, keepdims=True)
        va = jnp.concatenate([v.astype(BF16), ones], axis=-1)
        if init:
            m_ref[rows, :] = jnp.broadcast_to(mx, (span, LANES))
            pv = _dot(jnp.exp2(s - mx).astype(BF16), va)
            acc_ref[rows, :] = pv[:, :HEAD_DIM]
            l_ref[rows, :] = pv[:, HEAD_DIM:]
        else:
            m_old = m_ref[rows, :]
            m_new = jnp.maximum(m_old, mx)
            alpha = jnp.exp2(m_old - m_new)
            pv = _dot(jnp.exp2(s - _lane_tile(m_new, 2)).astype(BF16), va)
            m_ref[rows, :] = m_new
            acc_ref[rows, :] = alpha * acc_ref[rows, :] + pv[:, :HEAD_DIM]
            l_ref[rows, :] = alpha * l_ref[rows, :] + pv[:, HEAD_DIM:]

    for bi, (window, dil) in enumerate(DSW_PAIRS):
        assert window // dil == span
        n_sub = tile // (span * dil)
        prev0 = tile - span * dil
        for r in range(dil):
            def rows_of(start, n=span):
                return pl.ds(start, n, stride=dil) if dil > 1 else pl.ds(start, n)
            k = jnp.concatenate([kp_ref[rows_of(prev0 + r), :], kc_ref[rows_of(r), :]], axis=0)
            v = jnp.concatenate([vp_ref[rows_of(prev0 + r), :], vc_ref[rows_of(r), :]], axis=0)
            unit(q_ref[rows_of(r), :], k, v, band_first, rows_of(r), bi == 0)
            for i in range(1, n_sub):
                st = r + span * dil * i
                unit(q_ref[rows_of(st), :], kc_ref[rows_of(st - span * dil, 2 * span), :],
                     vc_ref[rows_of(st - span * dil, 2 * span), :], band, rows_of(st), bi == 0)
    o_ref[...] = acc_ref[...] / l_ref[...]


def _dsw(qk, proj, v_col):
    b, s, _ = qk.shape
    tile = DSW_TILE
    cur = lambda off: pl.BlockSpec((None, tile, HEAD_DIM), lambda bi, h, t: (bi, t, off + h))
    prev = lambda off: pl.BlockSpec((None, tile, HEAD_DIM), lambda bi, h, t: (bi, jnp.maximum(t - 1, 0), off + h))
    return pl.pallas_call(
        _dsw_kernel,
        grid=(b, DSW_HEADS, s // tile),
        in_specs=[cur(0), prev(DSW_HEADS), cur(DSW_HEADS), prev(v_col), cur(v_col)],
        out_specs=cur(0),
        out_shape=jax.ShapeDtypeStruct((b, s, DSW_HEADS * HEAD_DIM), F32),
        scratch_shapes=[pltpu.VMEM((tile, HEAD_DIM), F32)] * 3,
        compiler_params=_cparams("parallel", "parallel", "arbitrary"),
        name="dsw",
    )(qk, qk, qk, proj, proj)


def _nsa_compress_kernel(x_ref, pos_ref, w1_ref, w2_ref, o_ref, xf_ref, shift_ref):
    st = NSA_CMP_STRIDE
    n = x_ref.shape[0] // st
    xf_ref[...] = x_ref[...].astype(F32)
    p_acc = None
    q_acc = None
    for l in range(st):
        xl = xf_ref[pl.ds(l, n, stride=st), :]
        pt = _dot((xl + pos_ref[l:l + 1, :]).astype(BF16), w1_ref[l])
        qt = _dot((xl + pos_ref[st + l:st + l + 1, :]).astype(BF16), w1_ref[st + l])
        p_acc = pt if p_acc is None else p_acc + pt
        q_acc = qt if q_acc is None else q_acc + qt
    shift_ref[0:n, :] = q_acc
    shift_ref[n:n + 8, :] = jnp.zeros((8, q_acc.shape[1]), F32)
    hid = p_acc + shift_ref[pl.ds(1, n), :]
    o_ref[...] = _dot(_silu(hid).astype(BF16), w2_ref[...]).astype(o_ref.dtype)


def _nsa_compress(src, col0, pos, w1, w2):
    b, s, _ = src.shape
    n = s // NSA_CMP_STRIDE
    hid = w1.shape[-1]
    return pl.pallas_call(
        _nsa_compress_kernel,
        grid=(b, NSA_KV_HEADS),
        in_specs=[pl.BlockSpec((None, s, HEAD_DIM), lambda bi, g: (bi, 0, col0 + g)),
                  pl.BlockSpec((NSA_CMP_BLOCK, HEAD_DIM), lambda bi, g: (0, 0)),
                  pl.BlockSpec((NSA_CMP_BLOCK, HEAD_DIM, hid), lambda bi, g: (0, 0, 0)),
                  pl.BlockSpec((hid, HEAD_DIM), lambda bi, g: (0, 0))],
        out_specs=pl.BlockSpec((None, None, n, HEAD_DIM), lambda bi, g: (bi, g, 0, 0)),
        out_shape=jax.ShapeDtypeStruct((b, NSA_KV_HEADS, n, HEAD_DIM), BF16),
        scratch_shapes=[pltpu.VMEM((s, HEAD_DIM), F32), pltpu.VMEM((n + 8, hid), F32)],
        compiler_params=_cparams("parallel", "parallel"),
        name="nsa_compress",
    )(src, pos, w1, w2)


NSA_ROWS = 64


def _stack_heads(q_ref, dst_ref, lanes=slice(0, HEAD_DIM)):
    tq = q_ref.shape[0]
    for h in range(NSA_HPG):
        dst_ref[h * tq:(h + 1) * tq, lanes] = q_ref[:, h * HEAD_DIM:(h + 1) * HEAD_DIM]


def _nsa_cmp_kernel(q_ref, kc_ref, vc_ref, ovt_ref, oc_ref, sel_ref, bias_ref, s_ref, p_ref, psum_ref):
    tq = q_ref.shape[0]
    ncp = kc_ref.shape[0]
    rb = NSA_ROWS
    t0 = pl.program_id(2) * tq
    tpos = t0 + lax.broadcasted_iota(jnp.int32, (tq, ncp), 0)
    cend = lax.broadcasted_iota(jnp.int32, (tq, ncp), 1) * NSA_CMP_STRIDE + (NSA_CMP_BLOCK - 1)
    bias_ref[...] = jnp.where(cend <= tpos, 0.0, NEG_INF)
    kc = kc_ref[...]
    vc = vc_ref[...]
    for h in range(NSA_HPG):
        sl = slice(h * HEAD_DIM, (h + 1) * HEAD_DIM)
        s_ref[h] = _dot_nt(q_ref[:, sl], kc)
        for rc in range(tq // rb):
            rows = pl.ds(rc * rb, rb)
            s = s_ref[h, rows, :] + bias_ref[rows, :]
            m = jnp.max(s, axis=-1, keepdims=True)
            e = jnp.exp2(s - m)
            inv = jnp.where(m > 0.5 * NEG_INF, 1.0 / jnp.sum(e, axis=-1, keepdims=True), 0.0)
            p = e * inv
            p_ref[h, rows, :] = p.astype(BF16)
            psum_ref[rows, :] = p if h == 0 else psum_ref[rows, :] + p
        oc_ref[:, sl] = _dot(p_ref[h], vc).astype(oc_ref.dtype)

    psum = psum_ref[...]
    p_hi = psum.astype(BF16)
    p_lo = (psum - p_hi.astype(F32)).astype(BF16)
    imp_t = _dot_nt(ovt_ref[...], p_hi) + _dot_nt(ovt_ref[...], p_lo)
    for cg in range(tq // LANES):
        jb = lax.broadcasted_iota(jnp.int32, (LANES, LANES), 0)
        tpos = t0 + cg * LANES + lax.broadcasted_iota(jnp.int32, (LANES, LANES), 1)
        cur = tpos // NSA_SLC_BLOCK
        forced = (jb == 0) | (jb == cur) | (jb == cur - 1)
        score = imp_t[:, cg * LANES:(cg + 1) * LANES] + jnp.where(forced, NSA_FORCE_BONUS, 0.0)
        score = jnp.where(jb <= cur, score, NEG_INF)
        taken = -3e38
        for _ in range(NSA_SLC_TOPK):
            top = jnp.max(score, axis=0, keepdims=True)
            idx = jnp.min(jnp.where(score == top, jb, LANES), axis=0, keepdims=True)
            score = jnp.where(jb == idx, taken, score)
        sel = jnp.where(score == taken, 0.0, NEG_INF)
        sel_ref[cg * LANES:(cg + 1) * LANES, :] = sel.T.astype(BF16)


NSA_KEY_TILE = 512


def _nsa_slc_kernel(q_ref, sel_ref, k_ref, v_ref, oh_ref, o_ref, qa_ref, s_ref, p_ref, acc_ref, m_ref):
    tq = q_ref.shape[0]
    tk = NSA_KEY_TILE
    rb = NSA_ROWS
    t0 = pl.program_id(2) * tq
    _stack_heads(q_ref, qa_ref)
    for h in range(NSA_HPG):
        qa_ref[h * tq:(h + 1) * tq, HEAD_DIM:] = sel_ref[...]
    acc_ref[...] = jnp.zeros_like(acc_ref)
    m_ref[...] = jnp.full(m_ref.shape, NEG_INF, F32)
    ones = jnp.ones((tk, HEAD_DIM), BF16)

    def tile(jt, on_diagonal):
        k0 = pl.multiple_of(jt * tk, tk)
        ka = jnp.concatenate([k_ref[pl.ds(k0, tk), :], oh_ref[pl.ds(k0, tk), :]], axis=-1)
        va = jnp.concatenate([v_ref[pl.ds(k0, tk), :], ones], axis=-1)
        for h in range(NSA_HPG):
            s_ref[h] = _dot_nt(qa_ref[h * tq:(h + 1) * tq, :], ka)
            for rc in range(tq // rb):
                rows = pl.ds(rc * rb, rb)
                s = s_ref[h, rows, :]
                if on_diagonal:
                    tpos = t0 + rc * rb + lax.broadcasted_iota(jnp.int32, (rb, tk), 0)
                    s = jnp.where(k0 + lax.broadcasted_iota(jnp.int32, (rb, tk), 1) <= tpos, s, NEG_INF)
                m_old = m_ref[h, rows, :]
                m_new = jnp.maximum(m_old, jnp.max(s, axis=-1, keepdims=True))
                p = jnp.exp2(s - _lane_tile(m_new, tk // LANES))
                m_ref[h, rows, :] = m_new
                acc_ref[h, rows, :] = _lane_tile(jnp.exp2(m_old - m_new), 2) * acc_ref[h, rows, :]
                p_ref[h, rows, :] = p.astype(BF16)
            acc_ref[h] += _dot(p_ref[h], va)

    n_below = t0 // tk

    def pair(jp, carry):
        tile(2 * jp, False)
        tile(2 * jp + 1, False)
        return carry

    lax.fori_loop(0, n_below // 2, pair, 0)

    @pl.when(n_below % 2 == 1)
    def _():
        tile(n_below - 1, False)

    tile(n_below, True)
    for h in range(NSA_HPG):
        o_ref[:, h * HEAD_DIM:(h + 1) * HEAD_DIM] = (acc_ref[h, :, :HEAD_DIM] / acc_ref[h, :, HEAD_DIM:]).astype(o_ref.dtype)


def _nsa_win_kernel(q_ref, k_ref, v_ref, o_ref, bias_ref, s_ref, p_ref):
    tq = q_ref.shape[0]
    nk = NSA_WINDOW + tq
    rb = NSA_ROWS
    t0 = pl.program_id(2) * tq
    start = pl.multiple_of(jnp.maximum(t0 - NSA_WINDOW, 0), tq)
    tpos = t0 + lax.broadcasted_iota(jnp.int32, (tq, nk), 0)
    kpos = start + lax.broadcasted_iota(jnp.int32, (tq, nk), 1)
    bias_ref[...] = jnp.where((kpos <= tpos) & (kpos > tpos - NSA_WINDOW), 0.0, NEG_INF)
    kt = k_ref[pl.ds(start, nk), :]
    va = jnp.concatenate([v_ref[pl.ds(start, nk), :], jnp.ones((nk, HEAD_DIM), BF16)], axis=-1)
    for h in range(NSA_HPG):
        sl = slice(h * HEAD_DIM, (h + 1) * HEAD_DIM)
        s_ref[h] = _dot_nt(q_ref[:, sl], kt)
        for rc in range(tq // rb):
            rows = pl.ds(rc * rb, rb)
            s = s_ref[h, rows, :] + bias_ref[rows, :]
            p_ref[h, rows, :] = jnp.exp2(s - jnp.max(s, axis=-1, keepdims=True)).astype(BF16)
        pv = _dot(p_ref[h], va)
        o_ref[:, sl] = (pv[:, :HEAD_DIM] / pv[:, HEAD_DIM:]).astype(o_ref.dtype)


def _nsa_overlap_t(n_cmp_pad, n_cmp):
    c_start = np.arange(n_cmp_pad)[None, :] * NSA_CMP_STRIDE
    s_start = np.arange(LANES)[:, None] * NSA_SLC_BLOCK
    ov = (c_start < s_start + NSA_SLC_BLOCK) & (c_start + NSA_CMP_BLOCK > s_start)
    ov &= (np.arange(n_cmp_pad) < n_cmp)[None, :]
    return jnp.asarray(ov, BF16)


def _nsa_attention(hd, pv, kc, vc, tq_cmp=512, tq_slc=512, tq_win=256):
    b, s, _ = hd.shape
    g = NSA_KV_HEADS
    qw = NSA_HPG * HEAD_DIM
    ncp = s // NSA_CMP_STRIDE
    nk_win = NSA_WINDOW + tq_win
    assert s // NSA_SLC_BLOCK <= LANES and s >= nk_win and NSA_KEY_TILE % tq_slc == 0
    grid = lambda tq: (b, g, s // tq)
    q_spec = lambda tq: pl.BlockSpec((None, tq, qw), lambda bi, gi, i: (bi, i, gi))
    seq_spec = lambda off: pl.BlockSpec((None, s, HEAD_DIM), lambda bi, gi, i: (bi, 0, off + gi))
    cmp_spec = pl.BlockSpec((None, None, ncp, HEAD_DIM), lambda bi, gi, i: (bi, gi, 0, 0))
    sel_spec = lambda tq: pl.BlockSpec((None, None, tq, LANES), lambda bi, gi, i: (bi, gi, i, 0))
    const_spec = lambda shape: pl.BlockSpec(shape, lambda bi, gi, i: (0, 0))
    out_sd = jax.ShapeDtypeStruct((b, s, NSA_HEADS * HEAD_DIM), BF16)
    sel_sd = jax.ShapeDtypeStruct((b, g, s, LANES), BF16)
    sem = ("parallel", "parallel", "arbitrary")
    k_slc, k_win = NSA_HEADS + g, NSA_HEADS + 2 * g
    v_slc, v_win = g, 2 * g
    block_onehot = jnp.asarray(np.arange(s)[:, None] // NSA_SLC_BLOCK == np.arange(LANES)[None, :], BF16)

    rows = NSA_HPG * tq_cmp
    o_c, sel = pl.pallas_call(
        _nsa_cmp_kernel, grid=grid(tq_cmp),
        in_specs=[q_spec(tq_cmp), cmp_spec, cmp_spec, const_spec((LANES, ncp))],
        out_specs=[q_spec(tq_cmp), sel_spec(tq_cmp)], out_shape=[out_sd, sel_sd],
        scratch_shapes=[pltpu.VMEM((tq_cmp, ncp), F32), pltpu.VMEM((NSA_HPG, tq_cmp, ncp), F32),
                        pltpu.VMEM((NSA_HPG, tq_cmp, ncp), BF16), pltpu.VMEM((tq_cmp, ncp), F32)],
        compiler_params=_cparams(*sem), name="nsa_cmp",
    )(hd, kc, vc, _nsa_overlap_t(ncp, ncp - 1))
    tq = tq_slc
    o_s = pl.pallas_call(
        _nsa_slc_kernel, grid=grid(tq),
        in_specs=[q_spec(tq), sel_spec(tq), seq_spec(k_slc), seq_spec(v_slc), const_spec((s, LANES))],
        out_specs=q_spec(tq), out_shape=out_sd,
        scratch_shapes=[pltpu.VMEM((NSA_HPG * tq, 2 * HEAD_DIM), BF16), pltpu.VMEM((NSA_HPG, tq, NSA_KEY_TILE), F32),
                        pltpu.VMEM((NSA_HPG, tq, NSA_KEY_TILE), BF16), pltpu.VMEM((NSA_HPG, tq, 2 * HEAD_DIM), F32),
                        pltpu.VMEM((NSA_HPG, tq, HEAD_DIM), F32)],
        compiler_params=_cparams(*sem), name="nsa_slc",
    )(hd, sel, hd, pv, block_onehot)
    rows = NSA_HPG * tq_win
    o_w = pl.pallas_call(
        _nsa_win_kernel, grid=grid(tq_win),
        in_specs=[q_spec(tq_win), seq_spec(k_win), seq_spec(v_win)],
        out_specs=q_spec(tq_win), out_shape=out_sd,
        scratch_shapes=[pltpu.VMEM((tq_win, nk_win), F32), pltpu.VMEM((NSA_HPG, tq_win, nk_win), F32),
                        pltpu.VMEM((NSA_HPG, tq_win, nk_win), BF16)],
        compiler_params=_cparams(*sem), name="nsa_win",
    )(hd, hd, pv)
    return o_c, o_s, o_w


def _row(v, width=None):
    v = v.reshape(1, -1).astype(F32)
    if width is not None and v.shape[1] < width:
        v = jnp.pad(v, ((0, 0), (0, width - v.shape[1])))
    return v


def _even_mixer(xf, b, s, gain, w_in, w_out, conv_w, a_log, dt_bias, gdn_norm, q_norm, k_norm, rope):
    gw = GDN_HEADS * HEAD_DIM
    dw = DSW_HEADS * HEAD_DIM
    o_small = 4 * gw
    o_dsw = o_small + 2 * GDN_HEADS
    w_plain = jnp.concatenate([w_in[:, :o_small], w_in[:, o_dsw + 2 * dw:]], axis=1).astype(BF16)
    w_heads = w_in[:, o_dsw:o_dsw + 2 * dw].astype(BF16)
    w_small = jnp.pad(w_in[:, o_small:o_dsw], ((0, 0), (0, LANES - 2 * GDN_HEADS))).astype(BF16)
    pad8 = lambda v: jnp.pad(_row(v), ((0, 0), (GDN_HEADS, LANES - 2 * GDN_HEADS)))
    head_gain = jnp.concatenate([jnp.tile(_row(q_norm) * Q_SCALE_LOG2, (1, DSW_HEADS)),
                                 jnp.tile(_row(k_norm), (1, DSW_HEADS))], axis=1)

    proj = _norm_proj(xf, gain, w_plain, tn=1024)
    qk, small = _norm_proj_heads(xf, gain, w_heads, head_gain, rope, F32, w_small, "gdn_gates",
                                 (pad8(a_log), pad8(dt_bias)))
    cw = conv_w.reshape(GDN_CONV, 3, gw).transpose(1, 0, 2)
    o_a = _gdn(proj.reshape(b, s, -1), small.reshape(b, s, LANES), cw, _row(gdn_norm))
    o_b = _dsw(qk.reshape(b, s, -1), proj.reshape(b, s, -1), o_small // HEAD_DIM)
    t = b * s
    return _proj_res(xf, [o_a.reshape(t, gw), o_b.reshape(t, dw)],
                     [w_out[:gw].astype(BF16), w_out[gw:].astype(BF16)])


def _odd_mixer(xf, b, s, gain, w_in, w_out, q_norm, k_norm, cmp_pos, cmp_w1, cmp_w2, rope):
    qw = NSA_HEADS * HEAD_DIM
    kvw = NSA_KV_HEADS * HEAD_DIM
    col = lambda i: slice(qw + i * kvw, qw + (i + 1) * kvw)
    w_heads = jnp.concatenate([w_in[:, :qw], w_in[:, col(0)], w_in[:, col(2)], w_in[:, col(4)]], axis=1).astype(BF16)
    w_plain = jnp.concatenate([w_in[:, col(1)], w_in[:, col(3)], w_in[:, col(5)]], axis=1).astype(BF16)
    w_gate = jnp.pad(w_in[:, qw + 6 * kvw:], ((0, 0), (0, LANES - 3 * NSA_HEADS))).astype(BF16)
    head_gain = jnp.concatenate([jnp.tile(_row(q_norm) * Q_SCALE_LOG2, (1, NSA_HEADS))]
                                + [jnp.tile(_row(k_norm[i]), (1, NSA_KV_HEADS)) for i in range(3)], axis=1)

    hd, gates = _norm_proj_heads(xf, gain, w_heads, head_gain, rope, BF16, w_gate, "sigmoid")
    hd = hd.reshape(b, s, -1)
    pv = _norm_proj(xf, gain, w_plain, out_dtype=BF16).reshape(b, s, -1)
    hid = cmp_w1.shape[-1]
    w1 = cmp_w1.reshape(2, NSA_CMP_BLOCK, HEAD_DIM, hid).astype(BF16)
    kc = _nsa_compress(hd, NSA_HEADS, _perm_head_dims(cmp_pos[0]), _perm_head_dims(w1[0], axis=1),
                       _perm_head_dims(cmp_w2[0].astype(BF16)))
    vc = _nsa_compress(pv, 0, cmp_pos[1], w1[1], cmp_w2[1].astype(BF16))
    o_c, o_s, o_w = _nsa_attention(hd, pv, kc, vc)
    t = b * s
    return _proj_res(xf, [o_c.reshape(t, qw), o_s.reshape(t, qw), o_w.reshape(t, qw)], [w_out.astype(BF16)], gates)


def kernel(x, mem, positions, ffn1_norm, ffn1_w_gu, ffn1_w_down, mix_norm, ev_w_in, ev_w_out, gdn_conv_w, gdn_a_log, gdn_dt_bias, gdn_out_norm, dsw_q_norm, dsw_k_norm, od_w_in, od_w_out, nsa_q_norm, nsa_k_norm, nsa_cmp_pos, nsa_cmp_w1, nsa_cmp_w2, xa_norm, xa_mem_norm, xa_w_q, xa_w_kv, xa_q_norm, xa_k_norm, xa_w_o, ffn2_norm, ffn2_w_gu, ffn2_w_down):
    b, s, d = x.shape
    depth = ffn1_norm.shape[0]
    rope = _rope_tables(positions)
    xf = x.reshape(b * s, d)
    for i in range(depth):
        xf = _ffn(xf, _row(ffn1_norm[i]), ffn1_w_gu[i].astype(BF16), ffn1_w_down[i].astype(BF16))
        if i % 2 == 0:
            e = i // 2
            xf = _even_mixer(xf, b, s, _row(mix_norm[i]), ev_w_in[e], ev_w_out[e], gdn_conv_w[e], gdn_a_log[e],
                             gdn_dt_bias[e], gdn_out_norm[e], dsw_q_norm[e], dsw_k_norm[e], rope)
        else:
            o = i // 2
            xf = _odd_mixer(xf, b, s, _row(mix_norm[i]), od_w_in[o], od_w_out[o], nsa_q_norm[o], nsa_k_norm[o],
                            nsa_cmp_pos[o], nsa_cmp_w1[o], nsa_cmp_w2[o], rope)
        xf = _cross_attention(xf.reshape(b, s, d), mem, _row(xa_norm[i]), _row(xa_mem_norm[i]),
                              xa_w_q[i].astype(BF16), xa_w_kv[i].astype(BF16), _row(xa_q_norm[i]),
                              _row(xa_k_norm[i]), xa_w_o[i].astype(BF16)).reshape(b * s, d)
        xf = _ffn(xf, _row(ffn2_norm[i]), ffn2_w_gu[i].astype(BF16), ffn2_w_down[i].astype(BF16))
    return xf.reshape(b, s, d)
```
